```python
import jax, jax.numpy as jnp
from jax import lax
import numpy as np

D_MODEL = 1024
BATCH = 4
SEQ = 8192
DEPTH = 1

M_HEADS = 4
M_DH = D_MODEL // 8
M_W = M_HEADS * M_DH
M_CHUNK = 64
CONV_K = 4
F_HEADS = 8
F_DH = D_MODEL // 16
F_W = F_HEADS * F_DH
Q_BLOCK = 128
D_MIX = M_W + F_W
EPS = 1e-6

SPLITS = (M_W, M_W, M_W, M_W, M_W, M_HEADS, M_HEADS,
          F_W, F_W, F_W, F_W, F_HEADS)
SPLIT_IDX = [int(s) for s in np.cumsum(SPLITS)[:-1]]
D_IN = int(sum(SPLITS))

kernel_name = "hymba_mlstm_fox_adaln_layer"


def rms_norm(x, g):
    xf = x.astype(jnp.float32)
    y = xf * lax.rsqrt(jnp.mean(xf * xf, axis=-1, keepdims=True) + EPS)
    return (y * g.astype(jnp.float32)).astype(x.dtype)


def head_layer_norm(x, g):
    xf = x.astype(jnp.float32)
    mu = jnp.mean(xf, axis=-1, keepdims=True)
    var = jnp.mean((xf - mu) ** 2, axis=-1, keepdims=True)
    y = (xf - mu) * lax.rsqrt(var + EPS)
    y = y.reshape(x.shape[:-2] + (x.shape[-2] * x.shape[-1],))
    return (y * g.astype(jnp.float32)).astype(x.dtype)


def causal_depthwise_conv(x, w, b):
    rhs = w.reshape(CONV_K, 1, x.shape[-1])
    y = lax.conv_general_dilated(x, rhs.astype(x.dtype), window_strides=(1,),
                                 padding=[(CONV_K - 1, 0)],
                                 dimension_numbers=('NWC', 'WIO', 'NWC'),
                                 feature_group_count=x.shape[-1])
    return y + b


def mlstm_chunkwise(q, k, v, ig, lf):
    B, H, S, d = q.shape
    L = M_CHUNK
    NC = S // L

    def to_chunks(a):
        return jnp.moveaxis(a.reshape(a.shape[:2] + (NC, L) + a.shape[3:]), 2, 0)

    qc, kc, vc = to_chunks(q), to_chunks(k), to_chunks(v)
    ic = to_chunks(ig)
    bc = jnp.cumsum(to_chunks(lf), axis=-1)
    causal = jnp.tril(jnp.ones((L, L), dtype=bool))

    def step(carry, inp):
        C, n, m = carry
        qj, kj, vj, ij, bj = inp
        D = bj[..., :, None] - bj[..., None, :] + ij[..., None, :]
        D = jnp.where(causal, D, -jnp.inf)
        inter = bj + m[..., None]
        m_t = jnp.maximum(inter, jnp.max(D, axis=-1))
        w_inter = jnp.exp(inter - m_t)
        s = jnp.einsum('bhtd,bhsd->bhts', qj, kj) * jnp.exp(D - m_t[..., None])
        num = (w_inter[..., None] * jnp.einsum('bhtd,bhde->bhte', qj, C)
               + jnp.einsum('bhts,bhse->bhte', s, vj))
        den = w_inter * jnp.einsum('bhtd,bhd->bht', qj, n) + jnp.sum(s, axis=-1)
        h = num / jnp.maximum(jnp.abs(den), jnp.exp(-m_t))[..., None]
        bL = bj[..., -1]
        a = bL[..., None] - bj + ij
        m_new = jnp.maximum(bL + m, jnp.max(a, axis=-1))
        decay = jnp.exp(bL + m - m_new)
        ws = jnp.exp(a - m_new[..., None])
        C_new = decay[..., None, None] * C + jnp.einsum('bhs,bhsd,bhse->bhde', ws, kj, vj)
        n_new = decay[..., None] * n + jnp.einsum('bhs,bhsd->bhd', ws, kj)
        return (C_new, n_new, m_new), h

    init = (jnp.zeros((B, H, d, d), jnp.float32), jnp.zeros((B, H, d), jnp.float32),
            jnp.zeros((B, H), jnp.float32))
    _, hs = lax.scan(step, init, (qc, kc, vc, ic, bc))
    return hs.transpose(1, 0, 3, 2, 4).reshape(B, S, H, d)


def forgetting_attention(q, k, v, logf):
    B, H, S, d = q.shape
    NB = S // Q_BLOCK
    F = jnp.cumsum(logf, axis=-1)
    qb = q.reshape(B, H, NB, Q_BLOCK, d).transpose(2, 0, 1, 3, 4)
    Fb = F.reshape(B, H, NB, Q_BLOCK).transpose(2, 0, 1, 3)
    kpos = jnp.arange(S)
    scale = 1.0 / np.sqrt(d)

    def block(args):
        qi, Fi, i = args
        qpos = i * Q_BLOCK + jnp.arange(Q_BLOCK)
        logits = (jnp.einsum('bhqd,bhkd->bhqk', qi, k).astype(jnp.float32) * scale
                  + Fi[..., :, None] - F[..., None, :])
        logits = jnp.where(kpos[None, :] <= qpos[:, None], logits, -jnp.inf)
        p = jax.nn.softmax(logits, axis=-1)
        return jnp.einsum('bhqk,bhkd->bhqd', p.astype(v.dtype), v)

    out = lax.map(block, (qb, Fb, jnp.arange(NB)))
    return out.transpose(1, 0, 3, 2, 4).reshape(B, S, H, d)


def setup_inputs(seed: int = 0) -> dict:
    key = jax.random.key(seed)
    ks = jax.random.split(key, 16)
    f32 = jnp.float32
    x = jax.random.normal(ks[0], (BATCH, SEQ, D_MODEL), f32)
    c = jax.random.normal(ks[1], (BATCH, D_MODEL), f32)
    norm_g = 1.0 + 0.02 * jax.random.normal(ks[2], (DEPTH, D_MODEL), f32)
    w_ada = 0.5 * D_MODEL ** -0.5 * jax.random.normal(ks[3], (DEPTH, D_MODEL, 3 * D_MODEL), f32)
    b_ada = 0.02 * jax.random.normal(ks[4], (DEPTH, 3 * D_MODEL), f32)
    w_in = D_MODEL ** -0.5 * jax.random.normal(ks[5], (DEPTH, D_MODEL, D_IN), f32)
    conv_w = CONV_K ** -0.5 * jax.random.normal(ks[6], (DEPTH, CONV_K, 2 * M_W), f32)
    conv_b = 0.01 * jax.random.normal(ks[7], (DEPTH, 2 * M_W), f32)
    b_igate = 0.1 * jax.random.normal(ks[8], (DEPTH, M_HEADS), f32)
    b_fgate_m = (jnp.linspace(3.0, 6.0, M_HEADS, dtype=f32)[None, :]
                 + 0.1 * jax.random.normal(ks[9], (DEPTH, M_HEADS), f32))
    mlstm_norm_g = 1.0 + 0.02 * jax.random.normal(ks[10], (DEPTH, M_W), f32)
    b_fgate_f = (jnp.linspace(1.0, 5.0, F_HEADS, dtype=f32)[None, :]
                 + 0.1 * jax.random.normal(ks[11], (DEPTH, F_HEADS), f32))
    fox_qnorm_g = 1.0 + 0.02 * jax.random.normal(ks[12], (DEPTH, F_DH), f32)
    fox_knorm_g = 1.0 + 0.02 * jax.random.normal(ks[13], (DEPTH, F_DH), f32)
    w_out = D_MIX ** -0.5 * jax.random.normal(ks[14], (DEPTH, D_MIX, D_MODEL), f32)
    return {"x": x, "c": c, "norm_g": norm_g, "w_ada": w_ada, "b_ada": b_ada,
            "w_in": w_in, "conv_w": conv_w, "conv_b": conv_b, "b_igate": b_igate,
            "b_fgate_m": b_fgate_m, "mlstm_norm_g": mlstm_norm_g, "b_fgate_f": b_fgate_f,
            "fox_qnorm_g": fox_qnorm_g, "fox_knorm_g": fox_knorm_g, "w_out": w_out}


def reference(x, c, norm_g, w_ada, b_ada, w_in, conv_w, conv_b, b_igate, b_fgate_m,
              mlstm_norm_g, b_fgate_f, fox_qnorm_g, fox_knorm_g, w_out):
    B, S, _ = x.shape
    f32 = jnp.float32
    for l in range(DEPTH):
        mod = c @ w_ada[l] + b_ada[l]
        shift, scale, gate = jnp.split(mod, 3, axis=-1)
        h = rms_norm(x, norm_g[l]) * (1.0 + scale[:, None, :]) + shift[:, None, :]

        u = h @ w_in[l]
        (m_q, m_k, m_v, m_o, m_z, m_i, m_f,
         f_q, f_k, f_v, f_z, f_f) = jnp.split(u, SPLIT_IDX, axis=-1)

        qk = jax.nn.silu(causal_depthwise_conv(jnp.concatenate([m_q, m_k], axis=-1),
                                               conv_w[l], conv_b[l]))
        mq, mk = jnp.split(qk, 2, axis=-1)
        to_bhsd = lambda a, H, d: a.reshape(B, S, H, d).transpose(0, 2, 1, 3).astype(f32)
        mq = to_bhsd(mq, M_HEADS, M_DH)
        mk = to_bhsd(mk, M_HEADS, M_DH) * (1.0 / np.sqrt(M_DH))
        mv = to_bhsd(m_v, M_HEADS, M_DH)
        ig = (m_i.astype(f32) + b_igate[l].astype(f32)).transpose(0, 2, 1)
        lf = jax.nn.log_sigmoid(m_f.astype(f32) + b_fgate_m[l].astype(f32)).transpose(0, 2, 1)
        h_m = mlstm_chunkwise(mq, mk, mv, ig, lf)
        h_m = jax.nn.sigmoid(m_o.astype(f32)).reshape(B, S, M_HEADS, M_DH) * h_m
        y_m = head_layer_norm(h_m, mlstm_norm_g[l]).astype(x.dtype) * jax.nn.silu(m_z)

        fq = rms_norm(f_q.reshape(B, S, F_HEADS, F_DH), fox_qnorm_g[l]).transpose(0, 2, 1, 3)
        fk = rms_norm(f_k.reshape(B, S, F_HEADS, F_DH), fox_knorm_g[l]).transpose(0, 2, 1, 3)
        fv = f_v.reshape(B, S, F_HEADS, F_DH).transpose(0, 2, 1, 3)
        logf = jax.nn.log_sigmoid(f_f.astype(f32) + b_fgate_f[l].astype(f32)).transpose(0, 2, 1)
        h_f = forgetting_attention(fq, fk, fv, logf)
        y_f = h_f.reshape(B, S, F_W).astype(x.dtype) * jax.nn.silu(f_z)

        y = jnp.concatenate([y_m, y_f], axis=-1) @ w_out[l]
        x = x + gate[:, None, :] * y
    return x
```

```python
import functools
import math

import jax
import jax.numpy as jnp
from jax import lax
from jax.experimental import pallas as pl
from jax.experimental.pallas import tpu as pltpu

F32 = jnp.float32
BF16 = jnp.bfloat16

EPS = 1e-6
M_HEADS = 4
M_DH = 128
F_HEADS = 8
F_DH = 64
CONV_K = 4
M_W = M_HEADS * M_DH
F_W = F_HEADS * F_DH
GROUP_W = 512
N_GROUPS = 9
LANES = 128
SUBLANES = 8
LOG2E = 1.4426950408889634
NEG_BIG = -1e30

PROJ_ROWS = 512
M_CHUNK = 256
FOX_TQ = 256
FOX_TK = 256
VMEM_LIMIT = 56 * 1024 * 1024


def _dot(a, b):
    return jnp.dot(a, b, preferred_element_type=F32)


def _dot_nt(a, b):
    return lax.dot_general(a, b, (((1,), (1,)), ((), ())), preferred_element_type=F32)


def _dot_tn(a, b):
    return lax.dot_general(a, b, (((0,), (0,)), ((), ())), preferred_element_type=F32)


def _silu(x):
    return x * jax.nn.sigmoid(x)


def _log_sigmoid(x):
    return jnp.minimum(x, 0.0) - jnp.log1p(jnp.exp(-jnp.abs(x)))


def _adaln_kernel(c_ref, w_ref, b_ref, o_ref):
    c = c_ref[...]
    w = w_ref[...]
    c_hi = c.astype(BF16)
    c_lo = (c - c_hi.astype(F32)).astype(BF16)
    w_hi = w.astype(BF16)
    w_lo = (w - w_hi.astype(F32)).astype(BF16)
    acc = _dot(c_hi, w_hi) + _dot(c_hi, w_lo) + _dot(c_lo, w_hi)
    o_ref[...] = acc + b_ref[...]


def _adaln(c_pad, w_ada, b_ada):
    rows, d = c_pad.shape
    n = w_ada.shape[1]
    tn = d
    return pl.pallas_call(
        _adaln_kernel,
        grid=(n // tn,),
        in_specs=[
            pl.BlockSpec((rows, d), lambda j: (0, 0)),
            pl.BlockSpec((d, tn), lambda j: (0, j)),
            pl.BlockSpec((1, tn), lambda j: (0, j)),
        ],
        out_specs=pl.BlockSpec((rows, tn), lambda j: (0, j)),
        out_shape=jax.ShapeDtypeStruct((rows, n), F32),
        compiler_params=pltpu.CompilerParams(
            dimension_semantics=("arbitrary",), vmem_limit_bytes=VMEM_LIMIT),
        name="adaln",
    )(c_pad, w_ada, b_ada)


def _seg_cumsum(x, seg_row, n):
    shift = 1
    while shift < n:
        rolled = pltpu.roll(x, shift, axis=0)
        x = x + jnp.where(seg_row >= shift, rolled, 0.0)
        shift *= 2
    return x


def _proj_kernel(x_ref, mod_ref, ng_ref, w_ref, wg_ref, cw_ref, cb_ref, gb_ref, qg_ref, kg_ref,
                 hsum_ref, u_ref, gc_ref, gtm_ref, ft_ref, ext_ref, cum_ref):
    j = pl.program_id(1)
    ts = x_ref.shape[1]

    @pl.when(j == 0)
    def _():
        ext_ref[0:SUBLANES, :] = jnp.zeros((SUBLANES, ext_ref.shape[1]), F32)
        cum_ref[...] = jnp.zeros_like(cum_ref)

    x = x_ref[0]
    ms = jnp.mean(x * x, axis=-1, keepdims=True)
    xn = x * lax.rsqrt(ms + EPS) * ng_ref[...]
    shift = mod_ref[0, 0:1, :]
    scale = mod_ref[0, 1:2, :]
    h = (xn * (1.0 + scale) + shift).astype(BF16)

    def group(g):
        return _dot(h, w_ref[:, g * GROUP_W:(g + 1) * GROUP_W])

    ext_ref[SUBLANES:, 0:GROUP_W] = group(0)
    ext_ref[SUBLANES:, GROUP_W:] = group(1)
    conv = cb_ref[...] + cw_ref[0:1, :] * ext_ref[SUBLANES - 3:SUBLANES - 3 + ts, :]
    for t in range(1, CONV_K):
        lo = SUBLANES - 3 + t
        conv = conv + cw_ref[t:t + 1, :] * ext_ref[lo:lo + ts, :]
    ext_ref[0:SUBLANES, :] = ext_ref[ts:ts + SUBLANES, :]
    qk = _silu(conv)
    u_ref[0, :, 0:GROUP_W] = qk[:, 0:GROUP_W].astype(BF16)
    u_ref[0, :, GROUP_W:2 * GROUP_W] = (qk[:, GROUP_W:] * (1.0 / math.sqrt(M_DH))).astype(BF16)

    for g in (2, 3, 4, 7, 8):
        u_ref[0, :, g * GROUP_W:(g + 1) * GROUP_W] = group(g).astype(BF16)

    for g, gain_ref in ((5, qg_ref), (6, kg_ref)):
        u = group(g)
        ssq = _dot((u * u).astype(BF16), hsum_ref[...])
        y = u * lax.rsqrt(ssq * (1.0 / F_DH) + EPS) * gain_ref[...]
        u_ref[0, :, g * GROUP_W:(g + 1) * GROUP_W] = y.astype(BF16)

    ug = _dot(h, wg_ref[...]) + gb_ref[...]
    lane = lax.broadcasted_iota(jnp.int32, ug.shape, 1)
    row = lax.broadcasted_iota(jnp.int32, ug.shape, 0)
    lf = _log_sigmoid(ug)
    lf = jnp.where(lane >= 2 * M_HEADS, lf * LOG2E, lf)
    seg_row = jnp.where(lane < 2 * M_HEADS, row & (M_CHUNK - 1), row)
    cs = _seg_cumsum(lf, seg_row, ts)
    cs = cs + jnp.where(lane >= 2 * M_HEADS, cum_ref[0:1, :], 0.0)
    cum_ref[0:1, :] = cs[ts - 1:ts, :]
    gc = jnp.where(lane < M_HEADS, ug, cs)
    gc_ref[0] = gc
    gt = gc.T
    gtm_ref[0] = gt[0:2 * M_HEADS, :]
    for p in range(F_HEADS // 2):
        lo = 2 * M_HEADS + 2 * p
        ft_ref[0, p] = gt[lo:lo + 2, :]


def _proj(x, mod3, norm_g, w_big, w_gate, conv_w, conv_b, gate_b, qg, kg, hsum):
    b, s, d = x.shape
    ts = PROJ_ROWS
    n_big = w_big.shape[1]
    const = lambda *shape: pl.BlockSpec(shape, lambda bi, j: (0,) * len(shape))
    return pl.pallas_call(
        _proj_kernel,
        grid=(b, s // ts),
        in_specs=[
            pl.BlockSpec((1, ts, d), lambda bi, j: (bi, j, 0)),
            pl.BlockSpec((1, 3, d), lambda bi, j: (bi, 0, 0)),
            const(1, d),
            const(d, n_big),
            const(d, LANES),
            const(CONV_K, 2 * M_W),
            const(1, 2 * M_W),
            const(1, LANES),
            const(1, F_W),
            const(1, F_W),
            const(F_W, F_W),
        ],
        out_specs=[
            pl.BlockSpec((1, ts, n_big), lambda bi, j: (bi, j, 0)),
            pl.BlockSpec((1, ts, LANES), lambda bi, j: (bi, j, 0)),
            pl.BlockSpec((1, 2 * M_HEADS, ts), lambda bi, j: (bi, 0, j)),
            pl.BlockSpec((1, F_HEADS // 2, 2, ts), lambda bi, j: (bi, 0, 0, j)),
        ],
        out_shape=[
            jax.ShapeDtypeStruct((b, s, n_big), BF16),
            jax.ShapeDtypeStruct((b, s, LANES), F32),
            jax.ShapeDtypeStruct((b, 2 * M_HEADS, s), F32),
            jax.ShapeDtypeStruct((b, F_HEADS // 2, 2, s), F32),
        ],
        scratch_shapes=[
            pltpu.VMEM((ts + SUBLANES, 2 * M_W), F32),
            pltpu.VMEM((SUBLANES, LANES), F32),
        ],
        compiler_params=pltpu.CompilerParams(
            dimension_semantics=("arbitrary", "arbitrary"), vmem_limit_bytes=VMEM_LIMIT),
        name="proj",
    )(x, mod3, norm_g, w_big, w_gate, conv_w, conv_b, gate_b, qg, kg, hsum)


def _mlstm_kernel(q_ref, k_ref, v_ref, o_ref, z_ref, gc_ref, gt_ref, lng_ref, y_ref, cn_ref, m_ref):
    c = pl.program_id(1)
    L = q_ref.shape[1]

    @pl.when(c == 0)
    def _():
        cn_ref[...] = jnp.zeros_like(cn_ref)
        m_ref[...] = jnp.zeros_like(m_ref)

    rows = lax.broadcasted_iota(jnp.int32, (L, L), 0)
    cols = lax.broadcasted_iota(jnp.int32, (L, L), 1)
    causal = cols <= rows
    ones_col = (lax.broadcasted_iota(jnp.int32, (L, M_DH), 1) == 0).astype(BF16)

    for hd in range(M_HEADS):
        sl = slice(hd * M_DH, (hd + 1) * M_DH)
        q = q_ref[0, :, sl]
        k = k_ref[0, :, sl]
        v = v_ref[0, :, sl]
        v1 = jnp.concatenate([v, ones_col], axis=1)
        ig_col = gc_ref[0, :, hd:hd + 1]
        b_col = gc_ref[0, :, M_HEADS + hd:M_HEADS + hd + 1]
        ig_row = gt_ref[0, hd:hd + 1, :]
        b_row = gt_ref[0, M_HEADS + hd:M_HEADS + hd + 1, :]
        m_prev = m_ref[hd:hd + 1, 0:1]
        cn = cn_ref[hd]

        r_row = ig_row - b_row
        dmat = jnp.where(causal, b_col + r_row, NEG_BIG)
        inter = b_col + m_prev
        m_t = jnp.maximum(inter, jnp.max(dmat, axis=-1, keepdims=True))
        w_inter = jnp.exp(inter - m_t)
        sm = _dot_nt(q, k) * jnp.exp(dmat - m_t)
        numden = w_inter * _dot(q, cn.astype(BF16)) + _dot(sm.astype(BF16), v1)
        num = numden[:, 0:M_DH]
        den = numden[:, M_DH:M_DH + 1]
        hval = num / jnp.maximum(jnp.abs(den), jnp.exp(-m_t))

        b_last = b_row[:, L - 1:L]
        a_row = b_last + r_row
        m_new = jnp.maximum(b_last + m_prev, jnp.max(a_row, axis=-1, keepdims=True))
        decay = jnp.exp(b_last + m_prev - m_new)
        ws_col = jnp.exp(b_last - b_col + ig_col - m_new)
        kw = (k.astype(F32) * ws_col).astype(BF16)
        cn_ref[hd] = decay * cn + _dot_tn(kw, v1)
        m_ref[hd:hd + 1, :] = jnp.broadcast_to(m_new, (1, LANES))

        hm = jax.nn.sigmoid(o_ref[0, :, sl].astype(F32)) * hval
        mu = jnp.mean(hm, axis=-1, keepdims=True)
        dv = hm - mu
        var = jnp.mean(dv * dv, axis=-1, keepdims=True)
        y = dv * lax.rsqrt(var + EPS) * lng_ref[:, sl]
        y_ref[0, :, sl] = (y * _silu(z_ref[0, :, sl].astype(F32))).astype(BF16)


def _mlstm(u, gc, gtm, ln_g):
    b, s, _ = u.shape
    L = M_CHUNK
    col = lambda g: pl.BlockSpec((1, L, GROUP_W), lambda bi, c, g=g: (bi, c, g))
    return pl.pallas_call(
        _mlstm_kernel,
        grid=(b, s // L),
        in_specs=[
            col(0), col(1), col(2), col(3), col(4),
            pl.BlockSpec((1, L, LANES), lambda bi, c: (bi, c, 0)),
            pl.BlockSpec((1, 2 * M_HEADS, L), lambda bi, c: (bi, 0, c)),
            pl.BlockSpec((1, M_W), lambda bi, c: (0, 0)),
        ],
        out_specs=pl.BlockSpec((1, L, M_W), lambda bi, c: (bi, c, 0)),
        out_shape=jax.ShapeDtypeStruct((b, s, M_W), BF16),
        scratch_shapes=[
            pltpu.VMEM((M_HEADS, M_DH, 2 * M_DH), F32),
            pltpu.VMEM((SUBLANES, LANES), F32),
        ],
        compiler_params=pltpu.CompilerParams(
            dimension_semantics=("arbitrary", "arbitrary"), vmem_limit_bytes=VMEM_LIMIT),
        name="mlstm",
    )(u, u, u, u, u, gc, gtm, ln_g)


def _fox_kernel(q_ref, k_ref, v_ref, z_ref, ft_ref, y_ref, m_ref, l_ref, acc_ref):
    i = pl.program_id(2)
    tq = q_ref.shape[1]
    tk = FOX_TK
    nsub = tq // tk

    q = q_ref[0]
    lane_q = lax.broadcasted_iota(jnp.int32, q.shape, 1)
    zero = jnp.zeros_like(q)
    q_heads = (jnp.where(lane_q < F_DH, q, zero), jnp.where(lane_q >= F_DH, q, zero))

    m_ref[...] = jnp.full_like(m_ref, NEG_BIG)
    l_ref[...] = jnp.zeros_like(l_ref)
    acc_ref[...] = jnp.zeros_like(acc_ref)

    q_start = i * tq
    rows = lax.broadcasted_iota(jnp.int32, (tq, tk), 0)
    cols = lax.broadcasted_iota(jnp.int32, (tq, tk), 1)
    lane_o = lax.broadcasted_iota(jnp.int32, (tq, LANES), 1)
    first_head = lane_o < F_DH

    def step(jk, masked):
        k_start = pl.multiple_of(jk * tk, tk)
        k = k_ref[0, pl.ds(k_start, tk), :]
        v = v_ref[0, pl.ds(k_start, tk), :]
        f_rows = ft_ref[0, 0, :, pl.ds(k_start, tk)]
        pv = []
        alpha = []
        for hd in range(2):
            f_ref = ft_ref[0, 0, hd:hd + 1, pl.ds(pl.multiple_of(q_start, tq), LANES)][:, 0:1]
            s = _dot_nt(q_heads[hd], k) + (f_ref - f_rows[hd:hd + 1, :])
            if masked:
                s = jnp.where(cols + k_start <= rows + q_start, s, NEG_BIG)
            m_old = m_ref[hd]
            m_new = jnp.maximum(m_old, jnp.max(s, axis=-1, keepdims=True))
            a = jnp.exp2(m_old - m_new)
            p = jnp.exp2(s - m_new[:, 0:1])
            l_ref[hd] = a * l_ref[hd] + jnp.sum(p, axis=-1, keepdims=True)
            m_ref[hd] = m_new
            pv.append(_dot(p.astype(BF16), v))
            alpha.append(a)
        acc_ref[...] = (acc_ref[...] * jnp.where(first_head, alpha[0], alpha[1])
                        + jnp.where(first_head, pv[0], pv[1]))

    def body(jk, carry):
        step(jk, False)
        return carry

    lax.fori_loop(0, i * nsub, body, 0)
    for d in range(nsub):
        step(i * nsub + d, True)

    l_pair = jnp.where(first_head, l_ref[0], l_ref[1])
    out = acc_ref[...] / l_pair
    y_ref[0] = (out * _silu(z_ref[0].astype(F32))).astype(BF16)


def _fox(u, ft):
    b, s, _ = u.shape
    tq = FOX_TQ
    blocks_per_group = GROUP_W // LANES
    qb, kb, vb, zb = (g * blocks_per_group for g in (5, 6, 7, 8))
    return pl.pallas_call(
        _fox_kernel,
        grid=(b, F_HEADS // 2, s // tq),
        in_specs=[
            pl.BlockSpec((1, tq, LANES), lambda bi, p, i: (bi, i, qb + p)),
            pl.BlockSpec((1, s, LANES), lambda bi, p, i: (bi, 0, kb + p)),
            pl.BlockSpec((1, s, LANES), lambda bi, p, i: (bi, 0, vb + p)),
            pl.BlockSpec((1, tq, LANES), lambda bi, p, i: (bi, i, zb + p)),
            pl.BlockSpec((1, 1, 2, s), lambda bi, p, i: (bi, p, 0, 0)),
        ],
        out_specs=pl.BlockSpec((1, tq, LANES), lambda bi, p, i: (bi, i, p)),
        out_shape=jax.ShapeDtypeStruct((b, s, F_W), BF16),
        scratch_shapes=[
            pltpu.VMEM((2, tq, LANES), F32),
            pltpu.VMEM((2, tq, LANES), F32),
            pltpu.VMEM((tq, LANES), F32),
        ],
        compiler_params=pltpu.CompilerParams(
            dimension_semantics=("arbitrary", "arbitrary", "arbitrary"),
            vmem_limit_bytes=VMEM_LIMIT),
        name="fox",
    )(u, u, u, u, ft)


def _out_kernel(x_ref, mod_ref, ym_ref, yf_ref, w_ref, o_ref):
    y = _dot(ym_ref[0], w_ref[0:M_W, :]) + _dot(yf_ref[0], w_ref[M_W:, :])
    o_ref[0] = x_ref[0] + mod_ref[0, 2:3, :] * y


def _out(x, mod3, ym, yf, w_out):
    b, s, d = x.shape
    ts = PROJ_ROWS
    return pl.pallas_call(
        _out_kernel,
        grid=(b, s // ts),
        in_specs=[
            pl.BlockSpec((1, ts, d), lambda bi, j: (bi, j, 0)),
            pl.BlockSpec((1, 3, d), lambda bi, j: (bi, 0, 0)),
            pl.BlockSpec((1, ts, M_W), lambda bi, j: (bi, j, 0)),
            pl.BlockSpec((1, ts, F_W), lambda bi, j: (bi, j, 0)),
            pl.BlockSpec((M_W + F_W, d), lambda bi, j: (0, 0)),
        ],
        out_specs=pl.BlockSpec((1, ts, d), lambda bi, j: (bi, j, 0)),
        out_shape=jax.ShapeDtypeStruct((b, s, d), x.dtype),
        compiler_params=pltpu.CompilerParams(
            dimension_semantics=("arbitrary", "arbitrary"), vmem_limit_bytes=VMEM_LIMIT),
        name="out_proj",
    )(x, mod3, ym, yf, w_out)


def _layer(x, c_pad, norm_g, w_ada, b_ada, w_in, conv_w, conv_b, b_igate, b_fgate_m,
           mlstm_norm_g, b_fgate_f, fox_qnorm_g, fox_knorm_g, w_out):
    b, s, d = x.shape
    mod = _adaln(c_pad, w_ada, b_ada[None, :])
    mod3 = mod[:b].reshape(b, 3, d)

    n_m = 5 * M_W
    n_f = 4 * F_W
    w_big = jnp.concatenate(
        [w_in[:, :n_m], w_in[:, n_m + 2 * M_HEADS:n_m + 2 * M_HEADS + n_f]], axis=1).astype(BF16)
    w_gate = jnp.concatenate(
        [w_in[:, n_m:n_m + 2 * M_HEADS], w_in[:, n_m + 2 * M_HEADS + n_f:]], axis=1)
    n_gate = w_gate.shape[1]
    w_gate = jnp.pad(w_gate, ((0, 0), (0, LANES - n_gate))).astype(BF16)
    gate_b = jnp.pad(jnp.concatenate([b_igate, b_fgate_m, b_fgate_f]), (0, LANES - n_gate))[None, :]
    qg = jnp.tile(fox_qnorm_g, F_HEADS)[None, :] * (LOG2E / math.sqrt(F_DH))
    kg = jnp.tile(fox_knorm_g, F_HEADS)[None, :]
    head_id = jnp.arange(F_W) // F_DH
    hsum = (head_id[:, None] == head_id[None, :]).astype(BF16)

    u, gc, gtm, ft = _proj(x, mod3, norm_g[None, :], w_big, w_gate, conv_w, conv_b[None, :],
                           gate_b, qg, kg, hsum)
    ym = _mlstm(u, gc, gtm, mlstm_norm_g[None, :])
    yf = _fox(u, ft)
    return _out(x, mod3, ym, yf, w_out.astype(BF16))


def kernel(x, c, norm_g, w_ada, b_ada, w_in, conv_w, conv_b, b_igate, b_fgate_m, mlstm_norm_g,
           b_fgate_f, fox_qnorm_g, fox_knorm_g, w_out):
    depth = norm_g.shape[0]
    b = x.shape[0]
    c_pad = jnp.pad(c, ((0, (-b) % SUBLANES), (0, 0)))
    for l in range(depth):
        x = _layer(x, c_pad, norm_g[l], w_ada[l], b_ada[l], w_in[l], conv_w[l], conv_b[l],
                   b_igate[l], b_fgate_m[l], mlstm_norm_g[l], b_fgate_f[l], fox_qnorm_g[l],
                   fox_knorm_g[l], w_out[l])
    return x
```

```python
import functools
import math

import jax
import jax.numpy as jnp
from jax import lax
from jax.experimental import pallas as pl
from jax.experimental.pallas import tpu as pltpu

F32 = jnp.float32
BF16 = jnp.bfloat16

EPS = 1e-6
M_HEADS = 4
M_DH = 128
F_HEADS = 8
F_DH = 64
CONV_K = 4
M_W = M_HEADS * M_DH
F_W = F_HEADS * F_DH
GROUP_W = 512
N_GROUPS = 9
LANES = 128
SUBLANES = 8
LOG2E = 1.4426950408889634
NEG_BIG = -1e30

PROJ_ROWS = 512
M_CHUNK = 256
FOX_TQ = 512
FOX_TK = 512
VMEM_LIMIT = 56 * 1024 * 1024


def _dot(a, b):
    return jnp.dot(a, b, preferred_element_type=F32)


def _dot_nt(a, b):
    return lax.dot_general(a, b, (((1,), (1,)), ((), ())), preferred_element_type=F32)


def _dot_tn(a, b):
    return lax.dot_general(a, b, (((0,), (0,)), ((), ())), preferred_element_type=F32)


def _silu(x):
    return x * jax.nn.sigmoid(x)


def _log_sigmoid(x):
    return jnp.minimum(x, 0.0) - jnp.log1p(jnp.exp(-jnp.abs(x)))


def _adaln_kernel(c_ref, w_ref, b_ref, o_ref):
    c = c_ref[...]
    w = w_ref[...]
    c_hi = c.astype(BF16)
    c_lo = (c - c_hi.astype(F32)).astype(BF16)
    w_hi = w.astype(BF16)
    w_lo = (w - w_hi.astype(F32)).astype(BF16)
    acc = _dot(c_hi, w_hi) + _dot(c_hi, w_lo) + _dot(c_lo, w_hi)
    o_ref[...] = acc + b_ref[...]


def _adaln(c_pad, w_ada, b_ada):
    rows, d = c_pad.shape
    n = w_ada.shape[1]
    tn = d
    return pl.pallas_call(
        _adaln_kernel,
        grid=(n // tn,),
        in_specs=[
            pl.BlockSpec((rows, d), lambda j: (0, 0)),
            pl.BlockSpec((d, tn), lambda j: (0, j)),
            pl.BlockSpec((1, tn), lambda j: (0, j)),
        ],
        out_specs=pl.BlockSpec((rows, tn), lambda j: (0, j)),
        out_shape=jax.ShapeDtypeStruct((rows, n), F32),
        compiler_params=pltpu.CompilerParams(
            dimension_semantics=("arbitrary",), vmem_limit_bytes=VMEM_LIMIT),
        name="adaln",
    )(c_pad, w_ada, b_ada)


def _seg_cumsum(x, seg_row, n):
    shift = 1
    while shift < n:
        rolled = pltpu.roll(x, shift, axis=0)
        x = x + jnp.where(seg_row >= shift, rolled, 0.0)
        shift *= 2
    return x


def _proj_kernel(x_ref, mod_ref, ng_ref, w_ref, wg_ref, cw_ref, cb_ref, gb_ref, qg_ref, kg_ref,
                 hsum_ref, u_ref, gc_ref, gtm_ref, ft_ref, ext_ref, cum_ref):
    j = pl.program_id(1)
    ts = x_ref.shape[1]

    @pl.when(j == 0)
    def _():
        ext_ref[0:SUBLANES, :] = jnp.zeros((SUBLANES, ext_ref.shape[1]), F32)
        cum_ref[...] = jnp.zeros_like(cum_ref)

    x = x_ref[0]
    ms = jnp.mean(x * x, axis=-1, keepdims=True)
    xn = x * lax.rsqrt(ms + EPS) * ng_ref[...]
    shift = mod_ref[0, 0:1, :]
    scale = mod_ref[0, 1:2, :]
    h = (xn * (1.0 + scale) + shift).astype(BF16)

    def group(g):
        return _dot(h, w_ref[:, g * GROUP_W:(g + 1) * GROUP_W])

    ext_ref[SUBLANES:, 0:GROUP_W] = group(0)
    ext_ref[SUBLANES:, GROUP_W:] = group(1)
    conv = cb_ref[...] + cw_ref[0:1, :] * ext_ref[SUBLANES - 3:SUBLANES - 3 + ts, :]
    for t in range(1, CONV_K):
        lo = SUBLANES - 3 + t
        conv = conv + cw_ref[t:t + 1, :] * ext_ref[lo:lo + ts, :]
    ext_ref[0:SUBLANES, :] = ext_ref[ts:ts + SUBLANES, :]
    qk = _silu(conv)
    u_ref[0, :, 0:GROUP_W] = qk[:, 0:GROUP_W].astype(BF16)
    u_ref[0, :, GROUP_W:2 * GROUP_W] = (qk[:, GROUP_W:] * (1.0 / math.sqrt(M_DH))).astype(BF16)

    for g in (2, 3, 4, 7, 8):
        u_ref[0, :, g * GROUP_W:(g + 1) * GROUP_W] = group(g).astype(BF16)

    for g, gain_ref in ((5, qg_ref), (6, kg_ref)):
        u = group(g)
        ssq = _dot((u * u).astype(BF16), hsum_ref[...])
        y = u * lax.rsqrt(ssq * (1.0 / F_DH) + EPS) * gain_ref[...]
        u_ref[0, :, g * GROUP_W:(g + 1) * GROUP_W] = y.astype(BF16)

    ug = _dot(h, wg_ref[...]) + gb_ref[...]
    lane = lax.broadcasted_iota(jnp.int32, ug.shape, 1)
    row = lax.broadcasted_iota(jnp.int32, ug.shape, 0)
    lf = _log_sigmoid(ug)
    lf = jnp.where(lane >= 2 * M_HEADS, lf * LOG2E, lf)
    seg_row = jnp.where(lane < 2 * M_HEADS, row & (M_CHUNK - 1), row)
    cs = _seg_cumsum(lf, seg_row, ts)
    cs = cs + jnp.where(lane >= 2 * M_HEADS, cum_ref[0:1, :], 0.0)
    cum_ref[0:1, :] = cs[ts - 1:ts, :]
    gc = jnp.where(lane < M_HEADS, ug, cs)
    gc_ref[0] = gc
    gt = gc.T
    gtm_ref[0] = gt[0:2 * M_HEADS, :]
    for p in range(F_HEADS // 2):
        lo = 2 * M_HEADS + 2 * p
        ft_ref[0, p] = gt[lo:lo + 2, :]


def _proj(x, mod3, norm_g, w_big, w_gate, conv_w, conv_b, gate_b, qg, kg, hsum):
    b, s, d = x.shape
    ts = PROJ_ROWS
    n_big = w_big.shape[1]
    const = lambda *shape: pl.BlockSpec(shape, lambda bi, j: (0,) * len(shape))
    return pl.pallas_call(
        _proj_kernel,
        grid=(b, s // ts),
        in_specs=[
            pl.BlockSpec((1, ts, d), lambda bi, j: (bi, j, 0)),
            pl.BlockSpec((1, 3, d), lambda bi, j: (bi, 0, 0)),
            const(1, d),
            const(d, n_big),
            const(d, LANES),
            const(CONV_K, 2 * M_W),
            const(1, 2 * M_W),
            const(1, LANES),
            const(1, F_W),
            const(1, F_W),
            const(F_W, F_W),
        ],
        out_specs=[
            pl.BlockSpec((1, ts, n_big), lambda bi, j: (bi, j, 0)),
            pl.BlockSpec((1, ts, LANES), lambda bi, j: (bi, j, 0)),
            pl.BlockSpec((1, 2 * M_HEADS, ts), lambda bi, j: (bi, 0, j)),
            pl.BlockSpec((1, F_HEADS // 2, 2, ts), lambda bi, j: (bi, 0, 0, j)),
        ],
        out_shape=[
            jax.ShapeDtypeStruct((b, s, n_big), BF16),
            jax.ShapeDtypeStruct((b, s, LANES), F32),
            jax.ShapeDtypeStruct((b, 2 * M_HEADS, s), F32),
            jax.ShapeDtypeStruct((b, F_HEADS // 2, 2, s), F32),
        ],
        scratch_shapes=[
            pltpu.VMEM((ts + SUBLANES, 2 * M_W), F32),
            pltpu.VMEM((SUBLANES, LANES), F32),
        ],
        compiler_params=pltpu.CompilerParams(
            dimension_semantics=("arbitrary", "arbitrary"), vmem_limit_bytes=VMEM_LIMIT),
        name="proj",
    )(x, mod3, norm_g, w_big, w_gate, conv_w, conv_b, gate_b, qg, kg, hsum)


def _mlstm_kernel(q_ref, k_ref, v_ref, o_ref, z_ref, gc_ref, gt_ref, lng_ref, y_ref, cn_ref, m_ref):
    c = pl.program_id(1)
    L = q_ref.shape[1]

    @pl.when(c == 0)
    def _():
        cn_ref[...] = jnp.zeros_like(cn_ref)
        m_ref[...] = jnp.zeros_like(m_ref)

    rows = lax.broadcasted_iota(jnp.int32, (L, L), 0)
    cols = lax.broadcasted_iota(jnp.int32, (L, L), 1)
    causal = cols <= rows
    ones_col = (lax.broadcasted_iota(jnp.int32, (L, M_DH), 1) == 0).astype(BF16)

    for hd in range(M_HEADS):
        sl = slice(hd * M_DH, (hd + 1) * M_DH)
        q = q_ref[0, :, sl]
        k = k_ref[0, :, sl]
        v = v_ref[0, :, sl]
        v1 = jnp.concatenate([v, ones_col], axis=1)
        ig_col = gc_ref[0, :, hd:hd + 1]
        b_col = gc_ref[0, :, M_HEADS + hd:M_HEADS + hd + 1]
        ig_row = gt_ref[0, hd:hd + 1, :]
        b_row = gt_ref[0, M_HEADS + hd:M_HEADS + hd + 1, :]
        m_prev = m_ref[hd:hd + 1, 0:1]
        cn = cn_ref[hd]

        r_row = ig_row - b_row
        dmat = jnp.where(causal, b_col + r_row, NEG_BIG)
        inter = b_col + m_prev
        m_t = jnp.maximum(inter, jnp.max(dmat, axis=-1, keepdims=True))
        w_inter = jnp.exp(inter - m_t)
        sm = _dot_nt(q, k) * jnp.exp(dmat - m_t)
        numden = w_inter * _dot(q, cn.astype(BF16)) + _dot(sm.astype(BF16), v1)
        num = numden[:, 0:M_DH]
        den = numden[:, M_DH:M_DH + 1]
        hval = num / jnp.maximum(jnp.abs(den), jnp.exp(-m_t))

        b_last = b_row[:, L - 1:L]
        a_row = b_last + r_row
        m_new = jnp.maximum(b_last + m_prev, jnp.max(a_row, axis=-1, keepdims=True))
        decay = jnp.exp(b_last + m_prev - m_new)
        ws_col = jnp.exp(b_last - b_col + ig_col - m_new)
        kw = (k.astype(F32) * ws_col).astype(BF16)
        cn_ref[hd] = decay * cn + _dot_tn(kw, v1)
        m_ref[hd:hd + 1, :] = jnp.broadcast_to(m_new, (1, LANES))

        hm = jax.nn.sigmoid(o_ref[0, :, sl].astype(F32)) * hval
        mu = jnp.mean(hm, axis=-1, keepdims=True)
        dv = hm - mu
        var = jnp.mean(dv * dv, axis=-1, keepdims=True)
        y = dv * lax.rsqrt(var + EPS) * lng_ref[:, sl]
        y_ref[0, :, sl] = (y * _silu(z_ref[0, :, sl].astype(F32))).astype(BF16)


def _mlstm(u, gc, gtm, ln_g):
    b, s, _ = u.shape
    L = M_CHUNK
    col = lambda g: pl.BlockSpec((1, L, GROUP_W), lambda bi, c, g=g: (bi, c, g))
    return pl.pallas_call(
        _mlstm_kernel,
        grid=(b, s // L),
        in_specs=[
            col(0), col(1), col(2), col(3), col(4),
            pl.BlockSpec((1, L, LANES), lambda bi, c: (bi, c, 0)),
            pl.BlockSpec((1, 2 * M_HEADS, L), lambda bi, c: (bi, 0, c)),
            pl.BlockSpec((1, M_W), lambda bi, c: (0, 0)),
        ],
        out_specs=pl.BlockSpec((1, L, M_W), lambda bi, c: (bi, c, 0)),
        out_shape=jax.ShapeDtypeStruct((b, s, M_W), BF16),
        scratch_shapes=[
            pltpu.VMEM((M_HEADS, M_DH, 2 * M_DH), F32),
            pltpu.VMEM((SUBLANES, LANES), F32),
        ],
        compiler_params=pltpu.CompilerParams(
            dimension_semantics=("arbitrary", "arbitrary"), vmem_limit_bytes=VMEM_LIMIT),
        name="mlstm",
    )(u, u, u, u, u, gc, gtm, ln_g)


def _fox_kernel(q_ref, k_ref, v_ref, z_ref, ft_ref, y_ref, m_ref, accl_ref):
    i = pl.program_id(2)
    tq = q_ref.shape[1]
    tk = FOX_TK
    nsub = tq // tk

    q = q_ref[0]
    lane_q = lax.broadcasted_iota(jnp.int32, q.shape, 1)
    zero = jnp.zeros_like(q)
    q_heads = (jnp.where(lane_q < F_DH, q, zero), jnp.where(lane_q >= F_DH, q, zero))

    m_ref[...] = jnp.full_like(m_ref, NEG_BIG)
    accl_ref[...] = jnp.zeros_like(accl_ref)

    q_start = pl.multiple_of(i * tq, tq)
    f_q0 = ft_ref[0, 0, :, pl.ds(q_start, LANES)][:, 0:1]
    rows = lax.broadcasted_iota(jnp.int32, (tq, tk), 0)
    cols = lax.broadcasted_iota(jnp.int32, (tq, tk), 1)
    ones_blk = jnp.ones((tk, LANES), BF16)

    def step(jk, masked):
        k_start = pl.multiple_of(jk * tk, tk)
        k = k_ref[0, pl.ds(k_start, tk), :]
        v1 = jnp.concatenate([v_ref[0, pl.ds(k_start, tk), :], ones_blk], axis=1)
        bias = f_q0 - ft_ref[0, 0, :, pl.ds(k_start, tk)]
        for hd in range(2):
            s = _dot_nt(q_heads[hd], k) + bias[hd:hd + 1, :]
            if masked:
                s = jnp.where(cols + k_start <= rows + q_start, s, NEG_BIG)
            m_old = m_ref[hd]
            m_new = jnp.maximum(m_old, jnp.max(s, axis=-1, keepdims=True))
            a = jnp.exp2(m_old - m_new)
            p = jnp.exp2(s - jnp.tile(m_new, (1, tk // LANES)))
            m_ref[hd] = m_new
            accl_ref[hd] = jnp.tile(a, (1, 2)) * accl_ref[hd] + _dot(p.astype(BF16), v1)

    def body(jk, carry):
        step(jk, False)
        return carry

    lax.fori_loop(0, i * nsub, body, 0)
    for d in range(nsub):
        step(i * nsub + d, True)

    first_head = lax.broadcasted_iota(jnp.int32, (tq, LANES), 1) < F_DH
    acc = jnp.where(first_head, accl_ref[0, :, 0:LANES], accl_ref[1, :, 0:LANES])
    l_pair = jnp.where(first_head, accl_ref[0, :, LANES:], accl_ref[1, :, LANES:])
    y_ref[0] = (acc / l_pair * _silu(z_ref[0].astype(F32))).astype(BF16)


def _fox(u, ft):
    b, s, _ = u.shape
    tq = FOX_TQ
    blocks_per_group = GROUP_W // LANES
    qb, kb, vb, zb = (g * blocks_per_group for g in (5, 6, 7, 8))
    return pl.pallas_call(
        _fox_kernel,
        grid=(b, F_HEADS // 2, s // tq),
        in_specs=[
            pl.BlockSpec((1, tq, LANES), lambda bi, p, i: (bi, i, qb + p)),
            pl.BlockSpec((1, s, LANES), lambda bi, p, i: (bi, 0, kb + p)),
            pl.BlockSpec((1, s, LANES), lambda bi, p, i: (bi, 0, vb + p)),
            pl.BlockSpec((1, tq, LANES), lambda bi, p, i: (bi, i, zb + p)),
            pl.BlockSpec((1, 1, 2, s), lambda bi, p, i: (bi, p, 0, 0)),
        ],
        out_specs=pl.BlockSpec((1, tq, LANES), lambda bi, p, i: (bi, i, p)),
        out_shape=jax.ShapeDtypeStruct((b, s, F_W), BF16),
        scratch_shapes=[
            pltpu.VMEM((2, tq, LANES), F32),
            pltpu.VMEM((2, tq, 2 * LANES), F32),
        ],
        compiler_params=pltpu.CompilerParams(
            dimension_semantics=("arbitrary", "arbitrary", "arbitrary"),
            vmem_limit_bytes=VMEM_LIMIT),
        name="fox",
    )(u, u, u, u, ft)


def _out_kernel(x_ref, mod_ref, ym_ref, yf_ref, w_ref, o_ref):
    y = _dot(ym_ref[0], w_ref[0:M_W, :]) + _dot(yf_ref[0], w_ref[M_W:, :])
    o_ref[0] = x_ref[0] + mod_ref[0, 2:3, :] * y


def _out(x, mod3, ym, yf, w_out):
    b, s, d = x.shape
    ts = PROJ_ROWS
    return pl.pallas_call(
        _out_kernel,
        grid=(b, s // ts),
        in_specs=[
            pl.BlockSpec((1, ts, d), lambda bi, j: (bi, j, 0)),
            pl.BlockSpec((1, 3, d), lambda bi, j: (bi, 0, 0)),
            pl.BlockSpec((1, ts, M_W), lambda bi, j: (bi, j, 0)),
            pl.BlockSpec((1, ts, F_W), lambda bi, j: (bi, j, 0)),
            pl.BlockSpec((M_W + F_W, d), lambda bi, j: (0, 0)),
        ],
        out_specs=pl.BlockSpec((1, ts, d), lambda bi, j: (bi, j, 0)),
        out_shape=jax.ShapeDtypeStruct((b, s, d), x.dtype),
        compiler_params=pltpu.CompilerParams(
            dimension_semantics=("arbitrary", "arbitrary"), vmem_limit_bytes=VMEM_LIMIT),
        name="out_proj",
    )(x, mod3, ym, yf, w_out)


def _layer(x, c_pad, norm_g, w_ada, b_ada, w_in, conv_w, conv_b, b_igate, b_fgate_m,
           mlstm_norm_g, b_fgate_f, fox_qnorm_g, fox_knorm_g, w_out):
    b, s, d = x.shape
    mod = _adaln(c_pad, w_ada, b_ada[None, :])
    mod3 = mod[:b].reshape(b, 3, d)

    n_m = 5 * M_W
    n_f = 4 * F_W
    w_big = jnp.concatenate(
        [w_in[:, :n_m], w_in[:, n_m + 2 * M_HEADS:n_m + 2 * M_HEADS + n_f]], axis=1).astype(BF16)
    w_gate = jnp.concatenate(
        [w_in[:, n_m:n_m + 2 * M_HEADS], w_in[:, n_m + 2 * M_HEADS + n_f:]], axis=1)
    n_gate = w_gate.shape[1]
    w_gate = jnp.pad(w_gate, ((0, 0), (0, LANES - n_gate))).astype(BF16)
    gate_b = jnp.pad(jnp.concatenate([b_igate, b_fgate_m, b_fgate_f]), (0, LANES - n_gate))[None, :]
    qg = jnp.tile(fox_qnorm_g, F_HEADS)[None, :] * (LOG2E / math.sqrt(F_DH))
    kg = jnp.tile(fox_knorm_g, F_HEADS)[None, :]
    head_id = jnp.arange(F_W) // F_DH
    hsum = (head_id[:, None] == head_id[None, :]).astype(BF16)

    u, gc, gtm, ft = _proj(x, mod3, norm_g[None, :], w_big, w_gate, conv_w, conv_b[None, :],
                           gate_b, qg, kg, hsum)
    ym = _mlstm(u, gc, gtm, mlstm_norm_g[None, :])
    yf = _fox(u, ft)
    return _out(x, mod3, ym, yf, w_out.astype(BF16))


def kernel(x, c, norm_g, w_ada, b_ada, w_in, conv_w, conv_b, b_igate, b_fgate_m, mlstm_norm_g,
           b_fgate_f, fox_qnorm_g, fox_knorm_g, w_out):
    depth = norm_g.shape[0]
    b = x.shape[0]
    c_pad = jnp.pad(c, ((0, (-b) % SUBLANES), (0, 0)))
    for l in range(depth):
        x = _layer(x, c_pad, norm_g[l], w_ada[l], b_ada[l], w_in[l], conv_w[l], conv_b[l],
                   b_igate[l], b_fgate_m[l], mlstm_norm_g[l], b_fgate_f[l], fox_qnorm_g[l],
                   fox_knorm_g[l], w_out[l])
    return x
```

```python
import functools
import math

import jax
import jax.numpy as jnp
from jax import lax
from jax.experimental import pallas as pl
from jax.experimental.pallas import tpu as pltpu

F32 = jnp.float32
BF16 = jnp.bfloat16

EPS = 1e-6
M_HEADS = 4
M_DH = 128
F_HEADS = 8
F_DH = 64
CONV_K = 4
M_W = M_HEADS * M_DH
F_W = F_HEADS * F_DH
GROUP_W = 512
N_GROUPS = 9
LANES = 128
SUBLANES = 8
LOG2E = 1.4426950408889634
NEG_BIG = -1e30

PROJ_ROWS = 512
M_CHUNK = 256
FOX_TQ = 512
VMEM_LIMIT = 56 * 1024 * 1024


def _dot(a, b):
    return jnp.dot(a, b, preferred_element_type=F32)


def _dot_nt(a, b):
    return lax.dot_general(a, b, (((1,), (1,)), ((), ())), preferred_element_type=F32)


def _dot_tn(a, b):
    return lax.dot_general(a, b, (((0,), (0,)), ((), ())), preferred_element_type=F32)


def _silu(x):
    return x * jax.nn.sigmoid(x)


def _log_sigmoid(x):
    return jnp.minimum(x, 0.0) - jnp.log1p(jnp.exp(-jnp.abs(x)))


def _adaln_kernel(c_ref, w_ref, b_ref, o_ref):
    c = c_ref[...]
    w = w_ref[...]
    c_hi = c.astype(BF16)
    c_lo = (c - c_hi.astype(F32)).astype(BF16)
    w_hi = w.astype(BF16)
    w_lo = (w - w_hi.astype(F32)).astype(BF16)
    acc = _dot(c_hi, w_hi) + _dot(c_hi, w_lo) + _dot(c_lo, w_hi)
    o_ref[...] = acc + b_ref[...]


def _adaln(c_pad, w_ada, b_ada):
    rows, d = c_pad.shape
    n = w_ada.shape[1]
    tn = d
    return pl.pallas_call(
        _adaln_kernel,
        grid=(n // tn,),
        in_specs=[
            pl.BlockSpec((rows, d), lambda j: (0, 0)),
            pl.BlockSpec((d, tn), lambda j: (0, j)),
            pl.BlockSpec((1, tn), lambda j: (0, j)),
        ],
        out_specs=pl.BlockSpec((rows, tn), lambda j: (0, j)),
        out_shape=jax.ShapeDtypeStruct((rows, n), F32),
        compiler_params=pltpu.CompilerParams(
            dimension_semantics=("arbitrary",), vmem_limit_bytes=VMEM_LIMIT),
        name="adaln",
    )(c_pad, w_ada, b_ada)


def _seg_cumsum(x, seg_row, n):
    shift = 1
    while shift < n:
        rolled = pltpu.roll(x, shift, axis=0)
        x = x + jnp.where(seg_row >= shift, rolled, 0.0)
        shift *= 2
    return x


def _proj_kernel(x_ref, mod_ref, ng_ref, w_ref, wg_ref, cw_ref, cb_ref, gb_ref, qg_ref, kg_ref,
                 hsum_ref, u_ref, gc_ref, gtm_ref, ft_ref, ext_ref, cum_ref):
    j = pl.program_id(1)
    ts = x_ref.shape[1]

    @pl.when(j == 0)
    def _():
        ext_ref[0:SUBLANES, :] = jnp.zeros((SUBLANES, ext_ref.shape[1]), F32)
        cum_ref[...] = jnp.zeros_like(cum_ref)

    x = x_ref[0]
    ms = jnp.mean(x * x, axis=-1, keepdims=True)
    xn = x * lax.rsqrt(ms + EPS) * ng_ref[...]
    shift = mod_ref[0, 0:1, :]
    scale = mod_ref[0, 1:2, :]
    h = (xn * (1.0 + scale) + shift).astype(BF16)

    def group(g):
        return _dot(h, w_ref[:, g * GROUP_W:(g + 1) * GROUP_W])

    ext_ref[SUBLANES:, 0:GROUP_W] = group(0)
    ext_ref[SUBLANES:, GROUP_W:] = group(1)
    conv = cb_ref[...] + cw_ref[0:1, :] * ext_ref[SUBLANES - 3:SUBLANES - 3 + ts, :]
    for t in range(1, CONV_K):
        lo = SUBLANES - 3 + t
        conv = conv + cw_ref[t:t + 1, :] * ext_ref[lo:lo + ts, :]
    ext_ref[0:SUBLANES, :] = ext_ref[ts:ts + SUBLANES, :]
    qk = _silu(conv)
    u_ref[0, :, 0:GROUP_W] = qk[:, 0:GROUP_W].astype(BF16)
    u_ref[0, :, GROUP_W:2 * GROUP_W] = (qk[:, GROUP_W:] * (1.0 / math.sqrt(M_DH))).astype(BF16)

    for g in (2, 3, 4, 7, 8):
        u_ref[0, :, g * GROUP_W:(g + 1) * GROUP_W] = group(g).astype(BF16)

    for g, gain_ref in ((5, qg_ref), (6, kg_ref)):
        u = group(g)
        ssq = _dot((u * u).astype(BF16), hsum_ref[...])
        y = u * lax.rsqrt(ssq * (1.0 / F_DH) + EPS) * gain_ref[...]
        u_ref[0, :, g * GROUP_W:(g + 1) * GROUP_W] = y.astype(BF16)

    ug = _dot(h, wg_ref[...]) + gb_ref[...]
    lane = lax.broadcasted_iota(jnp.int32, ug.shape, 1)
    row = lax.broadcasted_iota(jnp.int32, ug.shape, 0)
    lf = _log_sigmoid(ug)
    lf = jnp.where(lane >= 2 * M_HEADS, lf * LOG2E, lf)
    seg_row = jnp.where(lane < 2 * M_HEADS, row & (M_CHUNK - 1), row)
    cs = _seg_cumsum(lf, seg_row, ts)
    cs = cs + jnp.where(lane >= 2 * M_HEADS, cum_ref[0:1, :], 0.0)
    cum_ref[0:1, :] = cs[ts - 1:ts, :]
    gc = jnp.where(lane < M_HEADS, ug, cs)
    gc_ref[0] = gc
    gt = gc.T
    gtm_ref[0] = gt[0:2 * M_HEADS, :]
    for p in range(F_HEADS // 2):
        lo = 2 * M_HEADS + 2 * p
        ft_ref[0, p] = gt[lo:lo + 2, :]


def _proj(x, mod3, norm_g, w_big, w_gate, conv_w, conv_b, gate_b, qg, kg, hsum):
    b, s, d = x.shape
    ts = PROJ_ROWS
    n_big = w_big.shape[1]
    const = lambda *shape: pl.BlockSpec(shape, lambda bi, j: (0,) * len(shape))
    return pl.pallas_call(
        _proj_kernel,
        grid=(b, s // ts),
        in_specs=[
            pl.BlockSpec((1, ts, d), lambda bi, j: (bi, j, 0)),
            pl.BlockSpec((1, 3, d), lambda bi, j: (bi, 0, 0)),
            const(1, d),
            const(d, n_big),
            const(d, LANES),
            const(CONV_K, 2 * M_W),
            const(1, 2 * M_W),
            const(1, LANES),
            const(1, F_W),
            const(1, F_W),
            const(F_W, F_W),
        ],
        out_specs=[
            pl.BlockSpec((1, ts, n_big), lambda bi, j: (bi, j, 0)),
            pl.BlockSpec((1, ts, LANES), lambda bi, j: (bi, j, 0)),
            pl.BlockSpec((1, 2 * M_HEADS, ts), lambda bi, j: (bi, 0, j)),
            pl.BlockSpec((1, F_HEADS // 2, 2, ts), lambda bi, j: (bi, 0, 0, j)),
        ],
        out_shape=[
            jax.ShapeDtypeStruct((b, s, n_big), BF16),
            jax.ShapeDtypeStruct((b, s, LANES), F32),
            jax.ShapeDtypeStruct((b, 2 * M_HEADS, s), F32),
            jax.ShapeDtypeStruct((b, F_HEADS // 2, 2, s), F32),
        ],
        scratch_shapes=[
            pltpu.VMEM((ts + SUBLANES, 2 * M_W), F32),
            pltpu.VMEM((SUBLANES, LANES), F32),
        ],
        compiler_params=pltpu.CompilerParams(
            dimension_semantics=("arbitrary", "arbitrary"), vmem_limit_bytes=VMEM_LIMIT),
        name="proj",
    )(x, mod3, norm_g, w_big, w_gate, conv_w, conv_b, gate_b, qg, kg, hsum)


def _mlstm_kernel(q_ref, k_ref, v_ref, o_ref, z_ref, gc_ref, gt_ref, lng_ref, y_ref, cn_ref, m_ref):
    c = pl.program_id(1)
    L = q_ref.shape[1]

    @pl.when(c == 0)
    def _():
        cn_ref[...] = jnp.zeros_like(cn_ref)
        m_ref[...] = jnp.zeros_like(m_ref)

    rows = lax.broadcasted_iota(jnp.int32, (L, L), 0)
    cols = lax.broadcasted_iota(jnp.int32, (L, L), 1)
    causal = cols <= rows
    ones_col = (lax.broadcasted_iota(jnp.int32, (L, M_DH), 1) == 0).astype(BF16)

    for hd in range(M_HEADS):
        sl = slice(hd * M_DH, (hd + 1) * M_DH)
        q = q_ref[0, :, sl]
        k = k_ref[0, :, sl]
        v = v_ref[0, :, sl]
        v1 = jnp.concatenate([v, ones_col], axis=1)
        ig_col = gc_ref[0, :, hd:hd + 1]
        b_col = gc_ref[0, :, M_HEADS + hd:M_HEADS + hd + 1]
        ig_row = gt_ref[0, hd:hd + 1, :]
        b_row = gt_ref[0, M_HEADS + hd:M_HEADS + hd + 1, :]
        m_prev = m_ref[hd:hd + 1, 0:1]
        cn = cn_ref[hd]

        r_row = ig_row - b_row
        dmat = jnp.where(causal, b_col + r_row, NEG_BIG)
        inter = b_col + m_prev
        m_t = jnp.maximum(inter, jnp.max(dmat, axis=-1, keepdims=True))
        w_inter = jnp.exp(inter - m_t)
        sm = _dot_nt(q, k) * jnp.exp(dmat - m_t)
        numden = w_inter * _dot(q, cn.astype(BF16)) + _dot(sm.astype(BF16), v1)
        num = numden[:, 0:M_DH]
        den = numden[:, M_DH:M_DH + 1]
        hval = num / jnp.maximum(jnp.abs(den), jnp.exp(-m_t))

        b_last = b_row[:, L - 1:L]
        a_row = b_last + r_row
        m_new = jnp.maximum(b_last + m_prev, jnp.max(a_row, axis=-1, keepdims=True))
        decay = jnp.exp(b_last + m_prev - m_new)
        ws_col = jnp.exp(b_last - b_col + ig_col - m_new)
        kw = (k.astype(F32) * ws_col).astype(BF16)
        cn_ref[hd] = decay * cn + _dot_tn(kw, v1)
        m_ref[hd:hd + 1, :] = jnp.broadcast_to(m_new, (1, LANES))

        hm = jax.nn.sigmoid(o_ref[0, :, sl].astype(F32)) * hval
        mu = jnp.mean(hm, axis=-1, keepdims=True)
        dv = hm - mu
        var = jnp.mean(dv * dv, axis=-1, keepdims=True)
        y = dv * lax.rsqrt(var + EPS) * lng_ref[:, sl]
        y_ref[0, :, sl] = (y * _silu(z_ref[0, :, sl].astype(F32))).astype(BF16)


def _mlstm(u, gc, gtm, ln_g):
    b, s, _ = u.shape
    L = M_CHUNK
    col = lambda g: pl.BlockSpec((1, L, GROUP_W), lambda bi, c, g=g: (bi, c, g))
    return pl.pallas_call(
        _mlstm_kernel,
        grid=(b, s // L),
        in_specs=[
            col(0), col(1), col(2), col(3), col(4),
            pl.BlockSpec((1, L, LANES), lambda bi, c: (bi, c, 0)),
            pl.BlockSpec((1, 2 * M_HEADS, L), lambda bi, c: (bi, 0, c)),
            pl.BlockSpec((1, M_W), lambda bi, c: (0, 0)),
        ],
        out_specs=pl.BlockSpec((1, L, M_W), lambda bi, c: (bi, c, 0)),
        out_shape=jax.ShapeDtypeStruct((b, s, M_W), BF16),
        scratch_shapes=[
            pltpu.VMEM((M_HEADS, M_DH, 2 * M_DH), F32),
            pltpu.VMEM((SUBLANES, LANES), F32),
        ],
        compiler_params=pltpu.CompilerParams(
            dimension_semantics=("arbitrary", "arbitrary"), vmem_limit_bytes=VMEM_LIMIT),
        name="mlstm",
    )(u, u, u, u, u, gc, gtm, ln_g)


def _fox_kernel(q_ref, k_ref, v_ref, z_ref, ft_ref, y_ref, s_ref, m_ref, al_ref, accl_ref):
    i = pl.program_id(2)
    tq = q_ref.shape[1]
    tk = tq

    q = q_ref[0]
    lane_q = lax.broadcasted_iota(jnp.int32, q.shape, 1)
    zero = jnp.zeros_like(q)
    q_heads = (jnp.where(lane_q < F_DH, q, zero), jnp.where(lane_q >= F_DH, q, zero))

    accl_ref[...] = jnp.zeros_like(accl_ref)

    q_start = pl.multiple_of(i * tq, tq)
    f_q0 = ft_ref[0, 0, :, pl.ds(q_start, LANES)][:, 0:1]
    ones_blk = jnp.ones((tk, LANES), BF16)

    def scores(jk, buf, prev, masked):
        k_start = pl.multiple_of(jk * tk, tk)
        k = k_ref[0, pl.ds(k_start, tk), :]
        bias = f_q0 - ft_ref[0, 0, :, pl.ds(k_start, tk)]
        for hd in range(2):
            s = _dot_nt(q_heads[hd], k) + bias[hd:hd + 1, :]
            if masked:
                rows = lax.broadcasted_iota(jnp.int32, (tq, tk), 0)
                cols = lax.broadcasted_iota(jnp.int32, (tq, tk), 1)
                s = jnp.where(cols <= rows, s, NEG_BIG)
            row_max = jnp.max(s, axis=-1, keepdims=True)
            if prev is None:
                m_new = jnp.broadcast_to(row_max, (tq, LANES))
                al_ref[buf, hd] = jnp.zeros((tq, LANES), F32)
            else:
                m_old = m_ref[prev, hd]
                m_new = jnp.maximum(m_old, row_max)
                al_ref[buf, hd] = jnp.exp2(m_old - m_new)
            m_ref[buf, hd] = m_new
            s_ref[buf, hd] = s

    def accumulate(jk, buf):
        k_start = pl.multiple_of(jk * tk, tk)
        v1 = jnp.concatenate([v_ref[0, pl.ds(k_start, tk), :], ones_blk], axis=1)
        for hd in range(2):
            p = jnp.exp2(s_ref[buf, hd] - jnp.tile(m_ref[buf, hd], (1, tk // LANES)))
            accl_ref[hd] = (jnp.tile(al_ref[buf, hd], (1, 2)) * accl_ref[hd]
                            + _dot(p.astype(BF16), v1))

    scores(i, 0, None, True)

    def pair(jj, carry):
        t = 2 * jj
        scores(t, 1, 0, False)
        accumulate(jnp.where(t == 0, i, t - 1), 0)
        scores(t + 1, 0, 1, False)
        accumulate(t, 1)
        return carry

    lax.fori_loop(0, i // 2, pair, 0)
    odd = i % 2 == 1

    @pl.when(odd)
    def _():
        scores(i - 1, 1, 0, False)

    last_even = (i // 2) * 2
    accumulate(jnp.where(last_even == 0, i, last_even - 1), 0)

    @pl.when(odd)
    def _():
        accumulate(i - 1, 1)

    first_head = lax.broadcasted_iota(jnp.int32, (tq, LANES), 1) < F_DH
    acc = jnp.where(first_head, accl_ref[0, :, 0:LANES], accl_ref[1, :, 0:LANES])
    l_pair = jnp.where(first_head, accl_ref[0, :, LANES:], accl_ref[1, :, LANES:])
    y_ref[0] = (acc / l_pair * _silu(z_ref[0].astype(F32))).astype(BF16)


def _fox(u, ft):
    b, s, _ = u.shape
    tq = FOX_TQ
    blocks_per_group = GROUP_W // LANES
    qb, kb, vb, zb = (g * blocks_per_group for g in (5, 6, 7, 8))
    return pl.pallas_call(
        _fox_kernel,
        grid=(b, F_HEADS // 2, s // tq),
        in_specs=[
            pl.BlockSpec((1, tq, LANES), lambda bi, p, i: (bi, i, qb + p)),
            pl.BlockSpec((1, s, LANES), lambda bi, p, i: (bi, 0, kb + p)),
            pl.BlockSpec((1, s, LANES), lambda bi, p, i: (bi, 0, vb + p)),
            pl.BlockSpec((1, tq, LANES), lambda bi, p, i: (bi, i, zb + p)),
            pl.BlockSpec((1, 1, 2, s), lambda bi, p, i: (bi, p, 0, 0)),
        ],
        out_specs=pl.BlockSpec((1, tq, LANES), lambda bi, p, i: (bi, i, p)),
        out_shape=jax.ShapeDtypeStruct((b, s, F_W), BF16),
        scratch_shapes=[
            pltpu.VMEM((2, 2, tq, tq), F32),
            pltpu.VMEM((2, 2, tq, LANES), F32),
            pltpu.VMEM((2, 2, tq, LANES), F32),
            pltpu.VMEM((2, tq, 2 * LANES), F32),
        ],
        compiler_params=pltpu.CompilerParams(
            dimension_semantics=("arbitrary", "arbitrary", "arbitrary"),
            vmem_limit_bytes=VMEM_LIMIT),
        name="fox",
    )(u, u, u, u, ft)


def _out_kernel(x_ref, mod_ref, ym_ref, yf_ref, w_ref, o_ref):
    y = _dot(ym_ref[0], w_ref[0:M_W, :]) + _dot(yf_ref[0], w_ref[M_W:, :])
    o_ref[0] = x_ref[0] + mod_ref[0, 2:3, :] * y


def _out(x, mod3, ym, yf, w_out):
    b, s, d = x.shape
    ts = PROJ_ROWS
    return pl.pallas_call(
        _out_kernel,
        grid=(b, s // ts),
        in_specs=[
            pl.BlockSpec((1, ts, d), lambda bi, j: (bi, j, 0)),
            pl.BlockSpec((1, 3, d), lambda bi, j: (bi, 0, 0)),
            pl.BlockSpec((1, ts, M_W), lambda bi, j: (bi, j, 0)),
            pl.BlockSpec((1, ts, F_W), lambda bi, j: (bi, j, 0)),
            pl.BlockSpec((M_W + F_W, d), lambda bi, j: (0, 0)),
        ],
        out_specs=pl.BlockSpec((1, ts, d), lambda bi, j: (bi, j, 0)),
        out_shape=jax.ShapeDtypeStruct((b, s, d), x.dtype),
        compiler_params=pltpu.CompilerParams(
            dimension_semantics=("arbitrary", "arbitrary"), vmem_limit_bytes=VMEM_LIMIT),
        name="out_proj",
    )(x, mod3, ym, yf, w_out)


def _layer(x, c_pad, norm_g, w_ada, b_ada, w_in, conv_w, conv_b, b_igate, b_fgate_m,
           mlstm_norm_g, b_fgate_f, fox_qnorm_g, fox_knorm_g, w_out):
    b, s, d = x.shape
    mod = _adaln(c_pad, w_ada, b_ada[None, :])
    mod3 = mod[:b].reshape(b, 3, d)

    n_m = 5 * M_W
    n_f = 4 * F_W
    w_big = jnp.concatenate(
        [w_in[:, :n_m], w_in[:, n_m + 2 * M_HEADS:n_m + 2 * M_HEADS + n_f]], axis=1).astype(BF16)
    w_gate = jnp.concatenate(
        [w_in[:, n_m:n_m + 2 * M_HEADS], w_in[:, n_m + 2 * M_HEADS + n_f:]], axis=1)
    n_gate = w_gate.shape[1]
    w_gate = jnp.pad(w_gate, ((0, 0), (0, LANES - n_gate))).astype(BF16)
    gate_b = jnp.pad(jnp.concatenate([b_igate, b_fgate_m, b_fgate_f]), (0, LANES - n_gate))[None, :]
    qg = jnp.tile(fox_qnorm_g, F_HEADS)[None, :] * (LOG2E / math.sqrt(F_DH))
    kg = jnp.tile(fox_knorm_g, F_HEADS)[None, :]
    head_id = jnp.arange(F_W) // F_DH
    hsum = (head_id[:, None] == head_id[None, :]).astype(BF16)

    u, gc, gtm, ft = _proj(x, mod3, norm_g[None, :], w_big, w_gate, conv_w, conv_b[None, :],
                           gate_b, qg, kg, hsum)
    ym = _mlstm(u, gc, gtm, mlstm_norm_g[None, :])
    yf = _fox(u, ft)
    return _out(x, mod3, ym, yf, w_out.astype(BF16))


def kernel(x, c, norm_g, w_ada, b_ada, w_in, conv_w, conv_b, b_igate, b_fgate_m, mlstm_norm_g,
           b_fgate_f, fox_qnorm_g, fox_knorm_g, w_out):
    depth = norm_g.shape[0]
    b = x.shape[0]
    c_pad = jnp.pad(c, ((0, (-b) % SUBLANES), (0, 0)))
    for l in range(depth):
        x = _layer(x, c_pad, norm_g[l], w_ada[l], b_ada[l], w_in[l], conv_w[l], conv_b[l],
                   b_igate[l], b_fgate_m[l], mlstm_norm_g[l], b_fgate_f[l], fox_qnorm_g[l],
                   fox_knorm_g[l], w_out[l])
    return x
```

```python
import functools
import math

import jax
import jax.numpy as jnp
from jax import lax
from jax.experimental import pallas as pl
from jax.experimental.pallas import tpu as pltpu

F32 = jnp.float32
BF16 = jnp.bfloat16

EPS = 1e-6
M_HEADS = 4
M_DH = 128
F_HEADS = 8
F_DH = 64
CONV_K = 4
M_W = M_HEADS * M_DH
F_W = F_HEADS * F_DH
GROUP_W = 512
W_MQ, W_MK, W_MV, W_MO, W_MZ, W_FQ, W_FK, W_FV, W_FZ = range(9)
U_MQ, U_MV, U_MO, U_MZ, U_FQ, U_FK, U_FV, U_FZ = range(8)
N_SLOTS = 8
LANES = 128
SUBLANES = 8
LOG2E = 1.4426950408889634
NEG_BIG = -1e30

PROJ_ROWS = 512
M_CHUNK = 256
FOX_TQ = 512
VMEM_LIMIT = 56 * 1024 * 1024


def _dot(a, b):
    return jnp.dot(a, b, preferred_element_type=F32)


def _dot_nt(a, b):
    return lax.dot_general(a, b, (((1,), (1,)), ((), ())), preferred_element_type=F32)


def _silu(x):
    return x * jax.nn.sigmoid(x)


def _log_sigmoid(x):
    return jnp.minimum(x, 0.0) - jnp.log1p(jnp.exp(-jnp.abs(x)))


def _adaln_kernel(c_ref, w_ref, b_ref, o_ref):
    c = c_ref[...]
    w = w_ref[...]
    c_hi = c.astype(BF16)
    c_lo = (c - c_hi.astype(F32)).astype(BF16)
    w_hi = w.astype(BF16)
    w_lo = (w - w_hi.astype(F32)).astype(BF16)
    acc = _dot(c_hi, w_hi) + _dot(c_hi, w_lo) + _dot(c_lo, w_hi)
    o_ref[...] = acc + b_ref[...]


def _adaln(c_pad, w_ada, b_ada):
    rows, d = c_pad.shape
    n = w_ada.shape[1]
    tn = d
    return pl.pallas_call(
        _adaln_kernel,
        grid=(n // tn,),
        in_specs=[
            pl.BlockSpec((rows, d), lambda j: (0, 0)),
            pl.BlockSpec((d, tn), lambda j: (0, j)),
            pl.BlockSpec((1, tn), lambda j: (0, j)),
        ],
        out_specs=pl.BlockSpec((rows, tn), lambda j: (0, j)),
        out_shape=jax.ShapeDtypeStruct((rows, n), F32),
        compiler_params=pltpu.CompilerParams(
            dimension_semantics=("arbitrary",), vmem_limit_bytes=VMEM_LIMIT),
        name="adaln",
    )(c_pad, w_ada, b_ada)


def _seg_cumsum(x, seg_row, n):
    shift = 1
    while shift < n:
        rolled = pltpu.roll(x, shift, axis=0)
        x = x + jnp.where(seg_row >= shift, rolled, 0.0)
        shift *= 2
    return x


def _proj_kernel(x_ref, mod_ref, ng_ref, w_ref, wg_ref, cw_ref, cb_ref, gb_ref, qg_ref, kg_ref,
                 hsum_ref, u_ref, kt_ref, gc_ref, gtm_ref, ft_ref, ext_ref, cum_ref):
    j = pl.program_id(1)
    ts = x_ref.shape[1]

    @pl.when(j == 0)
    def _():
        ext_ref[0:SUBLANES, :] = jnp.zeros((SUBLANES, ext_ref.shape[1]), F32)
        cum_ref[...] = jnp.zeros_like(cum_ref)

    x = x_ref[0]
    ms = jnp.mean(x * x, axis=-1, keepdims=True)
    xn = x * lax.rsqrt(ms + EPS) * ng_ref[...]
    shift = mod_ref[0, 0:1, :]
    scale = mod_ref[0, 1:2, :]
    h = (xn * (1.0 + scale) + shift).astype(BF16)

    def group(g):
        return _dot(h, w_ref[:, g * GROUP_W:(g + 1) * GROUP_W])

    def store(slot, val):
        u_ref[0, :, slot * GROUP_W:(slot + 1) * GROUP_W] = val.astype(BF16)

    ext_ref[SUBLANES:, 0:GROUP_W] = group(W_MQ)
    ext_ref[SUBLANES:, GROUP_W:] = group(W_MK)
    conv = cb_ref[...] + cw_ref[0:1, :] * ext_ref[SUBLANES - 3:SUBLANES - 3 + ts, :]
    for t in range(1, CONV_K):
        lo = SUBLANES - 3 + t
        conv = conv + cw_ref[t:t + 1, :] * ext_ref[lo:lo + ts, :]
    ext_ref[0:SUBLANES, :] = ext_ref[ts:ts + SUBLANES, :]
    qk = _silu(conv)
    store(U_MQ, qk[:, 0:GROUP_W])
    kt_ref[0] = (qk[:, GROUP_W:] * (1.0 / math.sqrt(M_DH))).T.astype(BF16)

    for g, slot in ((W_MV, U_MV), (W_MO, U_MO), (W_MZ, U_MZ), (W_FV, U_FV), (W_FZ, U_FZ)):
        store(slot, group(g))

    for g, slot, gain_ref in ((W_FQ, U_FQ, qg_ref), (W_FK, U_FK, kg_ref)):
        u = group(g)
        ssq = _dot((u * u).astype(BF16), hsum_ref[...])
        store(slot, u * lax.rsqrt(ssq * (1.0 / F_DH) + EPS) * gain_ref[...])

    ug = _dot(h, wg_ref[...]) + gb_ref[...]
    lane = lax.broadcasted_iota(jnp.int32, ug.shape, 1)
    row = lax.broadcasted_iota(jnp.int32, ug.shape, 0)
    lf = _log_sigmoid(ug)
    lf = jnp.where(lane >= 2 * M_HEADS, lf * LOG2E, lf)
    seg_row = jnp.where(lane < 2 * M_HEADS, row & (M_CHUNK - 1), row)
    cs = _seg_cumsum(lf, seg_row, ts)
    cs = cs + jnp.where(lane >= 2 * M_HEADS, cum_ref[0:1, :], 0.0)
    cum_ref[0:1, :] = cs[ts - 1:ts, :]
    gc = jnp.where(lane < M_HEADS, ug, cs)
    gc_ref[0] = gc
    gt = gc.T
    gtm_ref[0] = gt[0:2 * M_HEADS, :]
    for p in range(F_HEADS // 2):
        lo = 2 * M_HEADS + 2 * p
        ft_ref[0, p] = gt[lo:lo + 2, :]


def _proj(x, mod3, norm_g, w_big, w_gate, conv_w, conv_b, gate_b, qg, kg, hsum):
    b, s, d = x.shape
    ts = PROJ_ROWS
    n_big = w_big.shape[1]
    const = lambda *shape: pl.BlockSpec(shape, lambda bi, j: (0,) * len(shape))
    return pl.pallas_call(
        _proj_kernel,
        grid=(b, s // ts),
        in_specs=[
            pl.BlockSpec((1, ts, d), lambda bi, j: (bi, j, 0)),
            pl.BlockSpec((1, 3, d), lambda bi, j: (bi, 0, 0)),
            const(1, d),
            const(d, n_big),
            const(d, LANES),
            const(CONV_K, 2 * M_W),
            const(1, 2 * M_W),
            const(1, LANES),
            const(1, F_W),
            const(1, F_W),
            const(F_W, F_W),
        ],
        out_specs=[
            pl.BlockSpec((1, ts, N_SLOTS * GROUP_W), lambda bi, j: (bi, j, 0)),
            pl.BlockSpec((1, M_W, ts), lambda bi, j: (bi, 0, j)),
            pl.BlockSpec((1, ts, LANES), lambda bi, j: (bi, j, 0)),
            pl.BlockSpec((1, 2 * M_HEADS, ts), lambda bi, j: (bi, 0, j)),
            pl.BlockSpec((1, F_HEADS // 2, 2, ts), lambda bi, j: (bi, 0, 0, j)),
        ],
        out_shape=[
            jax.ShapeDtypeStruct((b, s, N_SLOTS * GROUP_W), BF16),
            jax.ShapeDtypeStruct((b, M_W, s), BF16),
            jax.ShapeDtypeStruct((b, s, LANES), F32),
            jax.ShapeDtypeStruct((b, 2 * M_HEADS, s), F32),
            jax.ShapeDtypeStruct((b, F_HEADS // 2, 2, s), F32),
        ],
        scratch_shapes=[
            pltpu.VMEM((ts + SUBLANES, 2 * M_W), F32),
            pltpu.VMEM((SUBLANES, LANES), F32),
        ],
        compiler_params=pltpu.CompilerParams(
            dimension_semantics=("arbitrary", "arbitrary"), vmem_limit_bytes=VMEM_LIMIT),
        name="proj",
    )(x, mod3, norm_g, w_big, w_gate, conv_w, conv_b, gate_b, qg, kg, hsum)


def _mlstm_kernel(q_ref, kt_ref, v_ref, o_ref, z_ref, gc_ref, gt_ref, lng_ref, y_ref, cn_ref, m_ref):
    c = pl.program_id(1)
    L = q_ref.shape[1]

    @pl.when(c == 0)
    def _():
        cn_ref[...] = jnp.zeros_like(cn_ref)
        m_ref[...] = jnp.zeros_like(m_ref)

    rows = lax.broadcasted_iota(jnp.int32, (L, L), 0)
    cols = lax.broadcasted_iota(jnp.int32, (L, L), 1)
    causal = cols <= rows
    ones_blk = jnp.ones((L, M_DH), BF16)
    gcol = gc_ref[0]

    for hd in range(M_HEADS):
        sl = slice(hd * M_DH, (hd + 1) * M_DH)
        q = q_ref[0, :, sl]
        kt = kt_ref[0, sl, :]
        v1 = jnp.concatenate([v_ref[0, :, sl], ones_blk], axis=1)
        b_c = jnp.broadcast_to(gcol[:, M_HEADS + hd:M_HEADS + hd + 1], (L, LANES))
        r_row = gt_ref[0, hd:hd + 1, :] - gt_ref[0, M_HEADS + hd:M_HEADS + hd + 1, :]
        m_prev = m_ref[hd:hd + 1, :]
        cn = cn_ref[hd]

        rmat = jnp.where(causal, jnp.broadcast_to(r_row, (L, L)), NEG_BIG)
        g = jnp.maximum(m_prev, jnp.max(rmat, axis=-1, keepdims=True))
        sm = _dot(q, kt) * jnp.exp(rmat - jnp.tile(g, (1, L // LANES)))
        w_inter = jnp.exp(m_prev - g)
        numden = jnp.tile(w_inter, (1, 2)) * _dot(q, cn.astype(BF16)) + _dot(sm.astype(BF16), v1)
        e_neg_m = jnp.exp(-(b_c + g))
        hval = numden[:, 0:M_DH] / jnp.maximum(jnp.abs(numden[:, M_DH:]), e_neg_m)

        g_last = g[L - 1:L, :]
        ws_row = jnp.exp(r_row - jnp.tile(g_last, (1, L // LANES)))
        ktw = (kt.astype(F32) * ws_row).astype(BF16)
        decay = jnp.exp(m_prev - g_last)
        cn_ref[hd] = jnp.tile(decay, (1, 2)) * cn + _dot(ktw, v1)
        m_ref[hd:hd + 1, :] = b_c[L - 1:L, :] + g_last

        hm = jax.nn.sigmoid(o_ref[0, :, sl].astype(F32)) * hval
        mu = jnp.mean(hm, axis=-1, keepdims=True)
        dv = hm - mu
        var = jnp.mean(dv * dv, axis=-1, keepdims=True)
        y = dv * lax.rsqrt(var + EPS) * lng_ref[:, sl]
        y_ref[0, :, sl] = (y * _silu(z_ref[0, :, sl].astype(F32))).astype(BF16)


def _mlstm(u, kt, gc, gtm, ln_g):
    b, s, _ = u.shape
    L = M_CHUNK
    col = lambda g: pl.BlockSpec((1, L, GROUP_W), lambda bi, c, g=g: (bi, c, g))
    return pl.pallas_call(
        _mlstm_kernel,
        grid=(b, s // L),
        in_specs=[
            col(U_MQ),
            pl.BlockSpec((1, M_W, L), lambda bi, c: (bi, 0, c)),
            col(U_MV), col(U_MO), col(U_MZ),
            pl.BlockSpec((1, L, LANES), lambda bi, c: (bi, c, 0)),
            pl.BlockSpec((1, 2 * M_HEADS, L), lambda bi, c: (bi, 0, c)),
            pl.BlockSpec((1, M_W), lambda bi, c: (0, 0)),
        ],
        out_specs=pl.BlockSpec((1, L, M_W), lambda bi, c: (bi, c, 0)),
        out_shape=jax.ShapeDtypeStruct((b, s, M_W), BF16),
        scratch_shapes=[
            pltpu.VMEM((M_HEADS, M_DH, 2 * M_DH), F32),
            pltpu.VMEM((SUBLANES, LANES), F32),
        ],
        compiler_params=pltpu.CompilerParams(
            dimension_semantics=("arbitrary", "arbitrary"), vmem_limit_bytes=VMEM_LIMIT),
        name="mlstm",
    )(u, kt, u, u, u, gc, gtm, ln_g)


def _fox_kernel(q_ref, k_ref, v_ref, z_ref, ft_ref, y_ref, s_ref, m_ref, al_ref, accl_ref):
    i = pl.program_id(2)
    tq = q_ref.shape[1]
    tk = tq

    q = q_ref[0]
    lane_q = lax.broadcasted_iota(jnp.int32, q.shape, 1)
    zero = jnp.zeros_like(q)
    q_heads = (jnp.where(lane_q < F_DH, q, zero), jnp.where(lane_q >= F_DH, q, zero))

    accl_ref[...] = jnp.zeros_like(accl_ref)

    q_start = pl.multiple_of(i * tq, tq)
    f_q0 = ft_ref[0, 0, :, pl.ds(q_start, LANES)][:, 0:1]
    ones_blk = jnp.ones((tk, LANES), BF16)

    def scores(jk, buf, prev, masked):
        k_start = pl.multiple_of(jk * tk, tk)
        k = k_ref[0, pl.ds(k_start, tk), :]
        bias = f_q0 - ft_ref[0, 0, :, pl.ds(k_start, tk)]
        for hd in range(2):
            s = _dot_nt(q_heads[hd], k) + bias[hd:hd + 1, :]
            if masked:
                rows = lax.broadcasted_iota(jnp.int32, (tq, tk), 0)
                cols = lax.broadcasted_iota(jnp.int32, (tq, tk), 1)
                s = jnp.where(cols <= rows, s, NEG_BIG)
            row_max = jnp.max(s, axis=-1, keepdims=True)
            if prev is None:
                m_new = jnp.broadcast_to(row_max, (tq, LANES))
                al_ref[buf, hd] = jnp.zeros((tq, LANES), F32)
            else:
                m_old = m_ref[prev, hd]
                m_new = jnp.maximum(m_old, row_max)
                al_ref[buf, hd] = jnp.exp2(m_old - m_new)
            m_ref[buf, hd] = m_new
            s_ref[buf, hd] = s

    def accumulate(jk, buf):
        k_start = pl.multiple_of(jk * tk, tk)
        v1 = jnp.concatenate([v_ref[0, pl.ds(k_start, tk), :], ones_blk], axis=1)
        for hd in range(2):
            p = jnp.exp2(s_ref[buf, hd] - jnp.tile(m_ref[buf, hd], (1, tk // LANES)))
            accl_ref[hd] = (jnp.tile(al_ref[buf, hd], (1, 2)) * accl_ref[hd]
                            + _dot(p.astype(BF16), v1))

    scores(i, 0, None, True)

    def pair(jj, carry):
        t = 2 * jj
        scores(t, 1, 0, False)
        accumulate(jnp.where(t == 0, i, t - 1), 0)
        scores(t + 1, 0, 1, False)
        accumulate(t, 1)
        return carry

    lax.fori_loop(0, i // 2, pair, 0)
    odd = i % 2 == 1

    @pl.when(odd)
    def _():
        scores(i - 1, 1, 0, False)

    last_even = (i // 2) * 2
    accumulate(jnp.where(last_even == 0, i, last_even - 1), 0)

    @pl.when(odd)
    def _():
        accumulate(i - 1, 1)

    first_head = lax.broadcasted_iota(jnp.int32, (tq, LANES), 1) < F_DH
    acc = jnp.where(first_head, accl_ref[0, :, 0:LANES], accl_ref[1, :, 0:LANES])
    l_pair = jnp.where(first_head, accl_ref[0, :, LANES:], accl_ref[1, :, LANES:])
    y_ref[0] = (acc / l_pair * _silu(z_ref[0].astype(F32))).astype(BF16)


def _fox(u, ft):
    b, s, _ = u.shape
    tq = FOX_TQ
    blocks_per_group = GROUP_W // LANES
    qb, kb, vb, zb = (slot * blocks_per_group for slot in (U_FQ, U_FK, U_FV, U_FZ))
    return pl.pallas_call(
        _fox_kernel,
        grid=(b, F_HEADS // 2, s // tq),
        in_specs=[
            pl.BlockSpec((1, tq, LANES), lambda bi, p, i: (bi, i, qb + p)),
            pl.BlockSpec((1, s, LANES), lambda bi, p, i: (bi, 0, kb + p)),
            pl.BlockSpec((1, s, LANES), lambda bi, p, i: (bi, 0, vb + p)),
            pl.BlockSpec((1, tq, LANES), lambda bi, p, i: (bi, i, zb + p)),
            pl.BlockSpec((1, 1, 2, s), lambda bi, p, i: (bi, p, 0, 0)),
        ],
        out_specs=pl.BlockSpec((1, tq, LANES), lambda bi, p, i: (bi, i, p)),
        out_shape=jax.ShapeDtypeStruct((b, s, F_W), BF16),
        scratch_shapes=[
            pltpu.VMEM((2, 2, tq, tq), F32),
            pltpu.VMEM((2, 2, tq, LANES), F32),
            pltpu.VMEM((2, 2, tq, LANES), F32),
            pltpu.VMEM((2, tq, 2 * LANES), F32),
        ],
        compiler_params=pltpu.CompilerParams(
            dimension_semantics=("arbitrary", "arbitrary", "arbitrary"),
            vmem_limit_bytes=VMEM_LIMIT),
        name="fox",
    )(u, u, u, u, ft)


def _out_kernel(x_ref, mod_ref, ym_ref, yf_ref, w_ref, o_ref):
    y = _dot(ym_ref[0], w_ref[0:M_W, :]) + _dot(yf_ref[0], w_ref[M_W:, :])
    o_ref[0] = x_ref[0] + mod_ref[0, 2:3, :] * y


def _out(x, mod3, ym, yf, w_out):
    b, s, d = x.shape
    ts = PROJ_ROWS
    return pl.pallas_call(
        _out_kernel,
        grid=(b, s // ts),
        in_specs=[
            pl.BlockSpec((1, ts, d), lambda bi, j: (bi, j, 0)),
            pl.BlockSpec((1, 3, d), lambda bi, j: (bi, 0, 0)),
            pl.BlockSpec((1, ts, M_W), lambda bi, j: (bi, j, 0)),
            pl.BlockSpec((1, ts, F_W), lambda bi, j: (bi, j, 0)),
            pl.BlockSpec((M_W + F_W, d), lambda bi, j: (0, 0)),
        ],
        out_specs=pl.BlockSpec((1, ts, d), lambda bi, j: (bi, j, 0)),
        out_shape=jax.ShapeDtypeStruct((b, s, d), x.dtype),
        compiler_params=pltpu.CompilerParams(
            dimension_semantics=("arbitrary", "arbitrary"), vmem_limit_bytes=VMEM_LIMIT),
        name="out_proj",
    )(x, mod3, ym, yf, w_out)


def _layer(x, c_pad, norm_g, w_ada, b_ada, w_in, conv_w, conv_b, b_igate, b_fgate_m,
           mlstm_norm_g, b_fgate_f, fox_qnorm_g, fox_knorm_g, w_out):
    b, s, d = x.shape
    mod = _adaln(c_pad, w_ada, b_ada[None, :])
    mod3 = mod[:b].reshape(b, 3, d)

    n_m = 5 * M_W
    n_f = 4 * F_W
    w_big = jnp.concatenate(
        [w_in[:, :n_m], w_in[:, n_m + 2 * M_HEADS:n_m + 2 * M_HEADS + n_f]], axis=1).astype(BF16)
    w_gate = jnp.concatenate(
        [w_in[:, n_m:n_m + 2 * M_HEADS], w_in[:, n_m + 2 * M_HEADS + n_f:]], axis=1)
    n_gate = w_gate.shape[1]
    w_gate = jnp.pad(w_gate, ((0, 0), (0, LANES - n_gate))).astype(BF16)
    gate_b = jnp.pad(jnp.concatenate([b_igate, b_fgate_m, b_fgate_f]), (0, LANES - n_gate))[None, :]
    qg = jnp.tile(fox_qnorm_g, F_HEADS)[None, :] * (LOG2E / math.sqrt(F_DH))
    kg = jnp.tile(fox_knorm_g, F_HEADS)[None, :]
    head_id = jnp.arange(F_W) // F_DH
    hsum = (head_id[:, None] == head_id[None, :]).astype(BF16)

    u, kt, gc, gtm, ft = _proj(x, mod3, norm_g[None, :], w_big, w_gate, conv_w, conv_b[None, :],
                           gate_b, qg, kg, hsum)
    ym = _mlstm(u, kt, gc, gtm, mlstm_norm_g[None, :])
    yf = _fox(u, ft)
    return _out(x, mod3, ym, yf, w_out.astype(BF16))


def kernel(x, c, norm_g, w_ada, b_ada, w_in, conv_w, conv_b, b_igate, b_fgate_m, mlstm_norm_g,
           b_fgate_f, fox_qnorm_g, fox_knorm_g, w_out):
    depth = norm_g.shape[0]
    b = x.shape[0]
    c_pad = jnp.pad(c, ((0, (-b) % SUBLANES), (0, 0)))
    for l in range(depth):
        x = _layer(x, c_pad, norm_g[l], w_ada[l], b_ada[l], w_in[l], conv_w[l], conv_b[l],
                   b_igate[l], b_fgate_m[l], mlstm_norm_g[l], b_fgate_f[l], fox_qnorm_g[l],
                   fox_knorm_g[l], w_out[l])
    return x
```

```python
import functools
import math

import jax
import jax.numpy as jnp
from jax import lax
from jax.experimental import pallas as pl
from jax.experimental.pallas import tpu as pltpu

F32 = jnp.float32
BF16 = jnp.bfloat16

EPS = 1e-6
M_HEADS = 4
M_DH = 128
F_HEADS = 8
F_DH = 64
CONV_K = 4
M_W = M_HEADS * M_DH
F_W = F_HEADS * F_DH
GROUP_W = 512
W_MQ, W_MK, W_MV, W_MO, W_MZ, W_FQ, W_FK, W_FV, W_FZ = range(9)
U_MQ, U_MV, U_MO, U_MZ, U_FQ, U_FK, U_FV, U_FZ = range(8)
N_SLOTS = 8
LANES = 128
SUBLANES = 8
LOG2E = 1.4426950408889634
NEG_BIG = -1e30

PROJ_ROWS = 512
M_CHUNK = 256
FOX_TQ = 512
VMEM_LIMIT = 56 * 1024 * 1024


def _dot(a, b):
    return jnp.dot(a, b, preferred_element_type=F32)


def _dot_nt(a, b):
    return lax.dot_general(a, b, (((1,), (1,)), ((), ())), preferred_element_type=F32)


def _silu(x):
    return x * jax.nn.sigmoid(x)


def _log_sigmoid(x):
    return jnp.minimum(x, 0.0) - jnp.log1p(jnp.exp(-jnp.abs(x)))


def _adaln_kernel(c_ref, w_ref, b_ref, o_ref):
    c = c_ref[...]
    w = w_ref[...]
    c_hi = c.astype(BF16)
    c_lo = (c - c_hi.astype(F32)).astype(BF16)
    w_hi = w.astype(BF16)
    w_lo = (w - w_hi.astype(F32)).astype(BF16)
    acc = _dot(c_hi, w_hi) + _dot(c_hi, w_lo) + _dot(c_lo, w_hi)
    o_ref[...] = acc + b_ref[...]


def _adaln(c_pad, w_ada, b_ada):
    rows, d = c_pad.shape
    n = w_ada.shape[1]
    tn = d
    return pl.pallas_call(
        _adaln_kernel,
        grid=(n // tn,),
        in_specs=[
            pl.BlockSpec((rows, d), lambda j: (0, 0)),
            pl.BlockSpec((d, tn), lambda j: (0, j)),
            pl.BlockSpec((1, tn), lambda j: (0, j)),
        ],
        out_specs=pl.BlockSpec((rows, tn), lambda j: (0, j)),
        out_shape=jax.ShapeDtypeStruct((rows, n), F32),
        compiler_params=pltpu.CompilerParams(
            dimension_semantics=("arbitrary",), vmem_limit_bytes=VMEM_LIMIT),
        name="adaln",
    )(c_pad, w_ada, b_ada)


def _seg_cumsum(x, seg_row, n):
    shift = 1
    while shift < n:
        rolled = pltpu.roll(x, shift, axis=0)
        x = x + jnp.where(seg_row >= shift, rolled, 0.0)
        shift *= 2
    return x


def _proj_kernel(x_ref, mod_ref, ng_ref, w_ref, wg_ref, cw_ref, cb_ref, gb_ref, qg_ref, kg_ref,
                 hsum_ref, u_ref, kt_ref, gc_ref, gtm_ref, ft_ref, ext_ref, cum_ref):
    j = pl.program_id(1)
    ts = x_ref.shape[1]

    @pl.when(j == 0)
    def _():
        ext_ref[0:SUBLANES, :] = jnp.zeros((SUBLANES, ext_ref.shape[1]), F32)
        cum_ref[...] = jnp.zeros_like(cum_ref)

    x = x_ref[0]
    ms = jnp.mean(x * x, axis=-1, keepdims=True)
    xn = x * lax.rsqrt(ms + EPS) * ng_ref[...]
    shift = mod_ref[0, 0:1, :]
    scale = mod_ref[0, 1:2, :]
    h = (xn * (1.0 + scale) + shift).astype(BF16)

    def group(g):
        return _dot(h, w_ref[:, g * GROUP_W:(g + 1) * GROUP_W])

    def store(slot, val):
        u_ref[0, :, slot * GROUP_W:(slot + 1) * GROUP_W] = val.astype(BF16)

    ext_ref[SUBLANES:, 0:GROUP_W] = group(W_MQ)
    ext_ref[SUBLANES:, GROUP_W:] = group(W_MK)
    conv = cb_ref[...] + cw_ref[0:1, :] * ext_ref[SUBLANES - 3:SUBLANES - 3 + ts, :]
    for t in range(1, CONV_K):
        lo = SUBLANES - 3 + t
        conv = conv + cw_ref[t:t + 1, :] * ext_ref[lo:lo + ts, :]
    ext_ref[0:SUBLANES, :] = ext_ref[ts:ts + SUBLANES, :]
    qk = _silu(conv)
    store(U_MQ, qk[:, 0:GROUP_W])
    kt_ref[0] = (qk[:, GROUP_W:] * (1.0 / math.sqrt(M_DH))).T.astype(BF16)

    for g, slot in ((W_MV, U_MV), (W_MO, U_MO), (W_MZ, U_MZ), (W_FV, U_FV), (W_FZ, U_FZ)):
        store(slot, group(g))

    for g, slot, gain_ref in ((W_FQ, U_FQ, qg_ref), (W_FK, U_FK, kg_ref)):
        u = group(g)
        ssq = _dot((u * u).astype(BF16), hsum_ref[...])
        store(slot, u * lax.rsqrt(ssq * (1.0 / F_DH) + EPS) * gain_ref[...])

    ug = _dot(h, wg_ref[...]) + gb_ref[...]
    lane = lax.broadcasted_iota(jnp.int32, ug.shape, 1)
    row = lax.broadcasted_iota(jnp.int32, ug.shape, 0)
    lf = _log_sigmoid(ug)
    lf = jnp.where(lane >= 2 * M_HEADS, lf * LOG2E, lf)
    seg_row = jnp.where(lane < 2 * M_HEADS, row & (M_CHUNK - 1), row)
    cs = _seg_cumsum(lf, seg_row, ts)
    cs = cs + jnp.where(lane >= 2 * M_HEADS, cum_ref[0:1, :], 0.0)
    cum_ref[0:1, :] = cs[ts - 1:ts, :]
    gc = jnp.where(lane < M_HEADS, ug, cs)
    gc_ref[0] = gc
    gt = gc.T
    gtm_ref[0] = gt[0:2 * M_HEADS, :]
    for p in range(F_HEADS // 2):
        lo = 2 * M_HEADS + 2 * p
        ft_ref[0, p] = gt[lo:lo + 2, :]


def _proj(x, mod3, norm_g, w_big, w_gate, conv_w, conv_b, gate_b, qg, kg, hsum):
    b, s, d = x.shape
    ts = PROJ_ROWS
    n_big = w_big.shape[1]
    const = lambda *shape: pl.BlockSpec(shape, lambda bi, j: (0,) * len(shape))
    return pl.pallas_call(
        _proj_kernel,
        grid=(b, s // ts),
        in_specs=[
            pl.BlockSpec((1, ts, d), lambda bi, j: (bi, j, 0)),
            pl.BlockSpec((1, 3, d), lambda bi, j: (bi, 0, 0)),
            const(1, d),
            const(d, n_big),
            const(d, LANES),
            const(CONV_K, 2 * M_W),
            const(1, 2 * M_W),
            const(1, LANES),
            const(1, F_W),
            const(1, F_W),
            const(F_W, F_W),
        ],
        out_specs=[
            pl.BlockSpec((1, ts, N_SLOTS * GROUP_W), lambda bi, j: (bi, j, 0)),
            pl.BlockSpec((1, M_W, ts), lambda bi, j: (bi, 0, j)),
            pl.BlockSpec((1, ts, LANES), lambda bi, j: (bi, j, 0)),
            pl.BlockSpec((1, 2 * M_HEADS, ts), lambda bi, j: (bi, 0, j)),
            pl.BlockSpec((1, F_HEADS // 2, 2, ts), lambda bi, j: (bi, 0, 0, j)),
        ],
        out_shape=[
            jax.ShapeDtypeStruct((b, s, N_SLOTS * GROUP_W), BF16),
            jax.ShapeDtypeStruct((b, M_W, s), BF16),
            jax.ShapeDtypeStruct((b, s, LANES), F32),
            jax.ShapeDtypeStruct((b, 2 * M_HEADS, s), F32),
            jax.ShapeDtypeStruct((b, F_HEADS // 2, 2, s), F32),
        ],
        scratch_shapes=[
            pltpu.VMEM((ts + SUBLANES, 2 * M_W), F32),
            pltpu.VMEM((SUBLANES, LANES), F32),
        ],
        compiler_params=pltpu.CompilerParams(
            dimension_semantics=("arbitrary", "arbitrary"), vmem_limit_bytes=VMEM_LIMIT),
        name="proj",
    )(x, mod3, norm_g, w_big, w_gate, conv_w, conv_b, gate_b, qg, kg, hsum)


def _mlstm_kernel(q_ref, kt_ref, v_ref, o_ref, z_ref, gc_ref, gt_ref, lng_ref, y_ref, cn_ref, m_ref):
    c = pl.program_id(1)
    L = q_ref.shape[1]

    @pl.when(c == 0)
    def _():
        cn_ref[...] = jnp.zeros_like(cn_ref)
        m_ref[...] = jnp.zeros_like(m_ref)

    rows = lax.broadcasted_iota(jnp.int32, (L, L), 0)
    cols = lax.broadcasted_iota(jnp.int32, (L, L), 1)
    causal = cols <= rows
    ones_blk = jnp.ones((L, M_DH), BF16)
    gcol = gc_ref[0]

    for hd in range(M_HEADS):
        sl = slice(hd * M_DH, (hd + 1) * M_DH)
        q = q_ref[0, :, sl]
        kt = kt_ref[0, sl, :]
        v1 = jnp.concatenate([v_ref[0, :, sl], ones_blk], axis=1)
        b_c = jnp.broadcast_to(gcol[:, M_HEADS + hd:M_HEADS + hd + 1], (L, LANES))
        r_row = gt_ref[0, hd:hd + 1, :] - gt_ref[0, M_HEADS + hd:M_HEADS + hd + 1, :]
        m_prev = m_ref[hd:hd + 1, :]
        cn = cn_ref[hd]

        rmat = jnp.where(causal, jnp.broadcast_to(r_row, (L, L)), NEG_BIG)
        g = jnp.maximum(m_prev, jnp.max(rmat, axis=-1, keepdims=True))
        sm = _dot(q, kt) * jnp.exp(rmat - jnp.tile(g, (1, L // LANES)))
        w_inter = jnp.exp(m_prev - g)
        numden = jnp.tile(w_inter, (1, 2)) * _dot(q, cn.astype(BF16)) + _dot(sm.astype(BF16), v1)
        e_neg_m = jnp.exp(-(b_c + g))
        hval = numden[:, 0:M_DH] / jnp.maximum(jnp.abs(numden[:, M_DH:]), e_neg_m)

        g_last = g[L - 1:L, :]
        ws_row = jnp.exp(r_row - jnp.tile(g_last, (1, L // LANES)))
        ktw = (kt.astype(F32) * ws_row).astype(BF16)
        decay = jnp.exp(m_prev - g_last)
        cn_ref[hd] = jnp.tile(decay, (1, 2)) * cn + _dot(ktw, v1)
        m_ref[hd:hd + 1, :] = b_c[L - 1:L, :] + g_last

        hm = jax.nn.sigmoid(o_ref[0, :, sl].astype(F32)) * hval
        mu = jnp.mean(hm, axis=-1, keepdims=True)
        dv = hm - mu
        var = jnp.mean(dv * dv, axis=-1, keepdims=True)
        y = dv * lax.rsqrt(var + EPS) * lng_ref[:, sl]
        y_ref[0, :, sl] = (y * _silu(z_ref[0, :, sl].astype(F32))).astype(BF16)


def _mlstm(u, kt, gc, gtm, ln_g):
    b, s, _ = u.shape
    L = M_CHUNK
    col = lambda g: pl.BlockSpec((1, L, GROUP_W), lambda bi, c, g=g: (bi, c, g))
    return pl.pallas_call(
        _mlstm_kernel,
        grid=(b, s // L),
        in_specs=[
            col(U_MQ),
            pl.BlockSpec((1, M_W, L), lambda bi, c: (bi, 0, c)),
            col(U_MV), col(U_MO), col(U_MZ),
            pl.BlockSpec((1, L, LANES), lambda bi, c: (bi, c, 0)),
            pl.BlockSpec((1, 2 * M_HEADS, L), lambda bi, c: (bi, 0, c)),
            pl.BlockSpec((1, M_W), lambda bi, c: (0, 0)),
        ],
        out_specs=pl.BlockSpec((1, L, M_W), lambda bi, c: (bi, c, 0)),
        out_shape=jax.ShapeDtypeStruct((b, s, M_W), BF16),
        scratch_shapes=[
            pltpu.VMEM((M_HEADS, M_DH, 2 * M_DH), F32),
            pltpu.VMEM((SUBLANES, LANES), F32),
        ],
        compiler_params=pltpu.CompilerParams(
            dimension_semantics=("arbitrary", "arbitrary"), vmem_limit_bytes=VMEM_LIMIT),
        name="mlstm",
    )(u, kt, u, u, u, gc, gtm, ln_g)


def _fox_kernel(q_ref, k_ref, v_ref, z_ref, ft_ref, y_ref, s_ref, mb_ref, al_ref, m_all, accl_all):
    T = FOX_TQ
    nq = q_ref.shape[1] // T
    n_main = nq * (nq - 1) // 2
    assert nq % 2 == 0 and n_main % 2 == 0 and n_main >= 4
    first_head = lax.broadcasted_iota(jnp.int32, (T, LANES), 1) < F_DH

    def block(ref, blk):
        return ref[0, pl.ds(pl.multiple_of(blk * T, T), T), :]

    def f_rows(blk):
        return ft_ref[0, 0, :, pl.ds(pl.multiple_of(blk * T, T), T)]

    def scores(i, jk, buf, diag):
        q = block(q_ref, i)
        zero = jnp.zeros_like(q)
        q_heads = (jnp.where(first_head, q, zero), jnp.where(first_head, zero, q))
        k = block(k_ref, jk)
        bias = f_rows(i)[:, 0:1] - f_rows(jk)
        for hd in range(2):
            s = _dot_nt(q_heads[hd], k) + bias[hd:hd + 1, :]
            if diag:
                rows = lax.broadcasted_iota(jnp.int32, (T, T), 0)
                cols = lax.broadcasted_iota(jnp.int32, (T, T), 1)
                s = jnp.where(cols <= rows, s, NEG_BIG)
            row_max = jnp.max(s, axis=-1, keepdims=True)
            if diag:
                m_new = jnp.broadcast_to(row_max, (T, LANES))
            else:
                m_old = m_all[i, hd]
                m_new = jnp.maximum(m_old, row_max)
                al_ref[buf, hd] = jnp.exp2(m_old - m_new)
            m_all[i, hd] = m_new
            mb_ref[buf, hd] = m_new
            s_ref[buf, hd] = s

    def accumulate(i, jk, buf, diag):
        v = block(v_ref, jk)
        one = jnp.ones_like(v)
        v_heads = (jnp.where(first_head, v, one), jnp.where(first_head, one, v))
        for hd in range(2):
            p = jnp.exp2(s_ref[buf, hd] - jnp.tile(mb_ref[buf, hd], (1, T // LANES)))
            pv = _dot(p.astype(BF16), v_heads[hd])
            if diag:
                accl_all[i, hd] = pv
            else:
                accl_all[i, hd] = al_ref[buf, hd] * accl_all[i, hd] + pv

    def advance(i, jk):
        wrap = jk + 1 >= i
        return jnp.where(wrap, i + 1, i), jnp.where(wrap, 0, jk + 1)

    scores(0, 0, 0, True)

    def diag_pair(jj, carry):
        i = 2 * jj
        scores(i + 1, i + 1, 1, True)
        accumulate(i, i, 0, True)
        scores(i + 2, i + 2, 0, True)
        accumulate(i + 1, i + 1, 1, True)
        return carry

    lax.fori_loop(0, nq // 2 - 1, diag_pair, 0)
    scores(nq - 1, nq - 1, 1, True)
    accumulate(nq - 2, nq - 2, 0, True)
    scores(1, 0, 0, False)
    accumulate(nq - 1, nq - 1, 1, True)

    def main_pair(_, carry):
        ia, ja = carry
        ib, jb = advance(ia, ja)
        scores(ib, jb, 1, False)
        accumulate(ia, ja, 0, False)
        ic, jc = advance(ib, jb)
        scores(ic, jc, 0, False)
        accumulate(ib, jb, 1, False)
        return ic, jc

    ia, ja = lax.fori_loop(0, n_main // 2 - 1, main_pair, (jnp.int32(1), jnp.int32(0)))
    ib, jb = advance(ia, ja)
    scores(ib, jb, 1, False)
    accumulate(ia, ja, 0, False)
    accumulate(ib, jb, 1, False)

    def finish(i, carry):
        outs = []
        for hd in range(2):
            a = accl_all[i, hd]
            outs.append(a * pltpu.roll(1.0 / a, F_DH, axis=1))
        out = jnp.where(first_head, outs[0], outs[1])
        rows = pl.ds(pl.multiple_of(i * T, T), T)
        y_ref[0, rows, :] = (out * _silu(z_ref[0, rows, :].astype(F32))).astype(BF16)
        return carry

    lax.fori_loop(0, nq, finish, 0)


def _fox(u, ft):
    b, s, _ = u.shape
    T = FOX_TQ
    nq = s // T
    blocks_per_group = GROUP_W // LANES
    qb, kb, vb, zb = (slot * blocks_per_group for slot in (U_FQ, U_FK, U_FV, U_FZ))
    seq = lambda first: pl.BlockSpec((1, s, LANES), lambda bi, p: (bi, 0, first + p))
    return pl.pallas_call(
        _fox_kernel,
        grid=(b, F_HEADS // 2),
        in_specs=[
            seq(qb), seq(kb), seq(vb), seq(zb),
            pl.BlockSpec((1, 1, 2, s), lambda bi, p: (bi, p, 0, 0)),
        ],
        out_specs=seq(0),
        out_shape=jax.ShapeDtypeStruct((b, s, F_W), BF16),
        scratch_shapes=[
            pltpu.VMEM((2, 2, T, T), F32),
            pltpu.VMEM((2, 2, T, LANES), F32),
            pltpu.VMEM((2, 2, T, LANES), F32),
            pltpu.VMEM((nq, 2, T, LANES), F32),
            pltpu.VMEM((nq, 2, T, LANES), F32),
        ],
        compiler_params=pltpu.CompilerParams(
            dimension_semantics=("arbitrary", "arbitrary"), vmem_limit_bytes=VMEM_LIMIT),
        name="fox",
    )(u, u, u, u, ft)


def _out_kernel(x_ref, mod_ref, ym_ref, yf_ref, w_ref, o_ref):
    y = _dot(ym_ref[0], w_ref[0:M_W, :]) + _dot(yf_ref[0], w_ref[M_W:, :])
    o_ref[0] = x_ref[0] + mod_ref[0, 2:3, :] * y


def _out(x, mod3, ym, yf, w_out):
    b, s, d = x.shape
    ts = PROJ_ROWS
    return pl.pallas_call(
        _out_kernel,
        grid=(b, s // ts),
        in_specs=[
            pl.BlockSpec((1, ts, d), lambda bi, j: (bi, j, 0)),
            pl.BlockSpec((1, 3, d), lambda bi, j: (bi, 0, 0)),
            pl.BlockSpec((1, ts, M_W), lambda bi, j: (bi, j, 0)),
            pl.BlockSpec((1, ts, F_W), lambda bi, j: (bi, j, 0)),
            pl.BlockSpec((M_W + F_W, d), lambda bi, j: (0, 0)),
        ],
        out_specs=pl.BlockSpec((1, ts, d), lambda bi, j: (bi, j, 0)),
        out_shape=jax.ShapeDtypeStruct((b, s, d), x.dtype),
        compiler_params=pltpu.CompilerParams(
            dimension_semantics=("arbitrary", "arbitrary"), vmem_limit_bytes=VMEM_LIMIT),
        name="out_proj",
    )(x, mod3, ym, yf, w_out)


def _layer(x, c_pad, norm_g, w_ada, b_ada, w_in, conv_w, conv_b, b_igate, b_fgate_m,
           mlstm_norm_g, b_fgate_f, fox_qnorm_g, fox_knorm_g, w_out):
    b, s, d = x.shape
    mod = _adaln(c_pad, w_ada, b_ada[None, :])
    mod3 = mod[:b].reshape(b, 3, d)

    n_m = 5 * M_W
    n_f = 4 * F_W
    w_big = jnp.concatenate(
        [w_in[:, :n_m], w_in[:, n_m + 2 * M_HEADS:n_m + 2 * M_HEADS + n_f]], axis=1).astype(BF16)
    w_gate = jnp.concatenate(
        [w_in[:, n_m:n_m + 2 * M_HEADS], w_in[:, n_m + 2 * M_HEADS + n_f:]], axis=1)
    n_gate = w_gate.shape[1]
    w_gate = jnp.pad(w_gate, ((0, 0), (0, LANES - n_gate))).astype(BF16)
    gate_b = jnp.pad(jnp.concatenate([b_igate, b_fgate_m, b_fgate_f]), (0, LANES - n_gate))[None, :]
    qg = jnp.tile(fox_qnorm_g, F_HEADS)[None, :] * (LOG2E / math.sqrt(F_DH))
    kg = jnp.tile(fox_knorm_g, F_HEADS)[None, :]
    head_id = jnp.arange(F_W) // F_DH
    hsum = (head_id[:, None] == head_id[None, :]).astype(BF16)

    u, kt, gc, gtm, ft = _proj(x, mod3, norm_g[None, :], w_big, w_gate, conv_w, conv_b[None, :],
                           gate_b, qg, kg, hsum)
    ym = _mlstm(u, kt, gc, gtm, mlstm_norm_g[None, :])
    yf = _fox(u, ft)
    return _out(x, mod3, ym, yf, w_out.astype(BF16))


def kernel(x, c, norm_g, w_ada, b_ada, w_in, conv_w, conv_b, b_igate, b_fgate_m, mlstm_norm_g,
           b_fgate_f, fox_qnorm_g, fox_knorm_g, w_out):
    depth = norm_g.shape[0]
    b = x.shape[0]
    c_pad = jnp.pad(c, ((0, (-b) % SUBLANES), (0, 0)))
    for l in range(depth):
        x = _layer(x, c_pad, norm_g[l], w_ada[l], b_ada[l], w_in[l], conv_w[l], conv_b[l],
                   b_igate[l], b_fgate_m[l], mlstm_norm_g[l], b_fgate_f[l], fox_qnorm_g[l],
                   fox_knorm_g[l], w_out[l])
    return x
```

```python
import functools
import math

import jax
import jax.numpy as jnp
from jax import lax
from jax.experimental import pallas as pl
from jax.experimental.pallas import tpu as pltpu

F32 = jnp.float32
BF16 = jnp.bfloat16

EPS = 1e-6
M_HEADS = 4
M_DH = 128
F_HEADS = 8
F_DH = 64
CONV_K = 4
M_W = M_HEADS * M_DH
F_W = F_HEADS * F_DH
GROUP_W = 512
W_MQ, W_MK, W_MV, W_MO, W_MZ, W_FQ, W_FK, W_FV, W_FZ = range(9)
U_MQ, U_MV, U_MO, U_MZ, U_FQ, U_FK, U_FV, U_FZ = range(8)
N_SLOTS = 8
LANES = 128
SUBLANES = 8
LOG2E = 1.4426950408889634
NEG_BIG = -1e30

PROJ_ROWS = 512
M_CHUNK = 256
FOX_TQ = 512
FOX_UNROLL = 4
VMEM_LIMIT = 56 * 1024 * 1024


def _dot(a, b):
    return jnp.dot(a, b, preferred_element_type=F32)


def _dot_nt(a, b):
    return lax.dot_general(a, b, (((1,), (1,)), ((), ())), preferred_element_type=F32)


def _silu(x):
    return x * jax.nn.sigmoid(x)


def _log_sigmoid(x):
    return jnp.minimum(x, 0.0) - jnp.log1p(jnp.exp(-jnp.abs(x)))


def _adaln_kernel(c_ref, w_ref, b_ref, o_ref):
    c = c_ref[...]
    w = w_ref[...]
    c_hi = c.astype(BF16)
    c_lo = (c - c_hi.astype(F32)).astype(BF16)
    w_hi = w.astype(BF16)
    w_lo = (w - w_hi.astype(F32)).astype(BF16)
    acc = _dot(c_hi, w_hi) + _dot(c_hi, w_lo) + _dot(c_lo, w_hi)
    o_ref[...] = acc + b_ref[...]


def _adaln(c_pad, w_ada, b_ada):
    rows, d = c_pad.shape
    n = w_ada.shape[1]
    tn = d
    return pl.pallas_call(
        _adaln_kernel,
        grid=(n // tn,),
        in_specs=[
            pl.BlockSpec((rows, d), lambda j: (0, 0)),
            pl.BlockSpec((d, tn), lambda j: (0, j)),
            pl.BlockSpec((1, tn), lambda j: (0, j)),
        ],
        out_specs=pl.BlockSpec((rows, tn), lambda j: (0, j)),
        out_shape=jax.ShapeDtypeStruct((rows, n), F32),
        compiler_params=pltpu.CompilerParams(
            dimension_semantics=("arbitrary",), vmem_limit_bytes=VMEM_LIMIT),
        name="adaln",
    )(c_pad, w_ada, b_ada)


def _seg_cumsum(x, seg_row, n):
    shift = 1
    while shift < n:
        rolled = pltpu.roll(x, shift, axis=0)
        x = x + jnp.where(seg_row >= shift, rolled, 0.0)
        shift *= 2
    return x


def _proj_kernel(x_ref, mod_ref, ng_ref, w_ref, wg_ref, cw_ref, cb_ref, gb_ref, qg_ref, kg_ref,
                 hsum_ref, u_ref, kt_ref, gc_ref, gtm_ref, ft_ref, ext_ref, cum_ref):
    j = pl.program_id(1)
    ts = x_ref.shape[1]

    @pl.when(j == 0)
    def _():
        ext_ref[0:SUBLANES, :] = jnp.zeros((SUBLANES, ext_ref.shape[1]), F32)
        cum_ref[...] = jnp.zeros_like(cum_ref)

    x = x_ref[0]
    ms = jnp.mean(x * x, axis=-1, keepdims=True)
    xn = x * lax.rsqrt(ms + EPS) * ng_ref[...]
    shift = mod_ref[0, 0:1, :]
    scale = mod_ref[0, 1:2, :]
    h = (xn * (1.0 + scale) + shift).astype(BF16)

    def group(g):
        return _dot(h, w_ref[:, g * GROUP_W:(g + 1) * GROUP_W])

    def store(slot, val):
        u_ref[0, :, slot * GROUP_W:(slot + 1) * GROUP_W] = val.astype(BF16)

    ext_ref[SUBLANES:, 0:GROUP_W] = group(W_MQ)
    ext_ref[SUBLANES:, GROUP_W:] = group(W_MK)
    conv = cb_ref[...] + cw_ref[0:1, :] * ext_ref[SUBLANES - 3:SUBLANES - 3 + ts, :]
    for t in range(1, CONV_K):
        lo = SUBLANES - 3 + t
        conv = conv + cw_ref[t:t + 1, :] * ext_ref[lo:lo + ts, :]
    ext_ref[0:SUBLANES, :] = ext_ref[ts:ts + SUBLANES, :]
    qk = _silu(conv)
    store(U_MQ, qk[:, 0:GROUP_W])
    kt_ref[0] = (qk[:, GROUP_W:] * (1.0 / math.sqrt(M_DH))).T.astype(BF16)

    for g, slot in ((W_MV, U_MV), (W_MO, U_MO), (W_MZ, U_MZ), (W_FV, U_FV), (W_FZ, U_FZ)):
        store(slot, group(g))

    for g, slot, gain_ref in ((W_FQ, U_FQ, qg_ref), (W_FK, U_FK, kg_ref)):
        u = group(g)
        ssq = _dot((u * u).astype(BF16), hsum_ref[...])
        store(slot, u * lax.rsqrt(ssq * (1.0 / F_DH) + EPS) * gain_ref[...])

    ug = _dot(h, wg_ref[...]) + gb_ref[...]
    lane = lax.broadcasted_iota(jnp.int32, ug.shape, 1)
    row = lax.broadcasted_iota(jnp.int32, ug.shape, 0)
    lf = _log_sigmoid(ug)
    lf = jnp.where(lane >= 2 * M_HEADS, lf * LOG2E, lf)
    seg_row = jnp.where(lane < 2 * M_HEADS, row & (M_CHUNK - 1), row)
    cs = _seg_cumsum(lf, seg_row, ts)
    cs = cs + jnp.where(lane >= 2 * M_HEADS, cum_ref[0:1, :], 0.0)
    cum_ref[0:1, :] = cs[ts - 1:ts, :]
    gc = jnp.where(lane < M_HEADS, ug, cs)
    gc_ref[0] = gc
    gt = gc.T
    gtm_ref[0] = gt[0:2 * M_HEADS, :]
    for p in range(F_HEADS // 2):
        lo = 2 * M_HEADS + 2 * p
        ft_ref[0, p] = gt[lo:lo + 2, :]


def _proj(x, mod3, norm_g, w_big, w_gate, conv_w, conv_b, gate_b, qg, kg, hsum):
    b, s, d = x.shape
    ts = PROJ_ROWS
    n_big = w_big.shape[1]
    const = lambda *shape: pl.BlockSpec(shape, lambda bi, j: (0,) * len(shape))
    return pl.pallas_call(
        _proj_kernel,
        grid=(b, s // ts),
        in_specs=[
            pl.BlockSpec((1, ts, d), lambda bi, j: (bi, j, 0)),
            pl.BlockSpec((1, 3, d), lambda bi, j: (bi, 0, 0)),
            const(1, d),
            const(d, n_big),
            const(d, LANES),
            const(CONV_K, 2 * M_W),
            const(1, 2 * M_W),
            const(1, LANES),
            const(1, F_W),
            const(1, F_W),
            const(F_W, F_W),
        ],
        out_specs=[
            pl.BlockSpec((1, ts, N_SLOTS * GROUP_W), lambda bi, j: (bi, j, 0)),
            pl.BlockSpec((1, M_W, ts), lambda bi, j: (bi, 0, j)),
            pl.BlockSpec((1, ts, LANES), lambda bi, j: (bi, j, 0)),
            pl.BlockSpec((1, 2 * M_HEADS, ts), lambda bi, j: (bi, 0, j)),
            pl.BlockSpec((1, F_HEADS // 2, 2, ts), lambda bi, j: (bi, 0, 0, j)),
        ],
        out_shape=[
            jax.ShapeDtypeStruct((b, s, N_SLOTS * GROUP_W), BF16),
            jax.ShapeDtypeStruct((b, M_W, s), BF16),
            jax.ShapeDtypeStruct((b, s, LANES), F32),
            jax.ShapeDtypeStruct((b, 2 * M_HEADS, s), F32),
            jax.ShapeDtypeStruct((b, F_HEADS // 2, 2, s), F32),
        ],
        scratch_shapes=[
            pltpu.VMEM((ts + SUBLANES, 2 * M_W), F32),
            pltpu.VMEM((SUBLANES, LANES), F32),
        ],
        compiler_params=pltpu.CompilerParams(
            dimension_semantics=("arbitrary", "arbitrary"), vmem_limit_bytes=VMEM_LIMIT),
        name="proj",
    )(x, mod3, norm_g, w_big, w_gate, conv_w, conv_b, gate_b, qg, kg, hsum)


def _mlstm_kernel(q_ref, kt_ref, v_ref, o_ref, z_ref, gc_ref, gt_ref, lng_ref, y_ref, cn_ref, m_ref):
    c = pl.program_id(1)
    L = q_ref.shape[1]

    @pl.when(c == 0)
    def _():
        cn_ref[...] = jnp.zeros_like(cn_ref)
        m_ref[...] = jnp.zeros_like(m_ref)

    rows = lax.broadcasted_iota(jnp.int32, (L, L), 0)
    cols = lax.broadcasted_iota(jnp.int32, (L, L), 1)
    causal = cols <= rows
    ones_blk = jnp.ones((L, M_DH), BF16)
    gcol = gc_ref[0]

    for hd in range(M_HEADS):
        sl = slice(hd * M_DH, (hd + 1) * M_DH)
        q = q_ref[0, :, sl]
        kt = kt_ref[0, sl, :]
        v1 = jnp.concatenate([v_ref[0, :, sl], ones_blk], axis=1)
        b_c = jnp.broadcast_to(gcol[:, M_HEADS + hd:M_HEADS + hd + 1], (L, LANES))
        r_row = gt_ref[0, hd:hd + 1, :] - gt_ref[0, M_HEADS + hd:M_HEADS + hd + 1, :]
        m_prev = m_ref[hd:hd + 1, :]
        cn = cn_ref[hd]

        rmat = jnp.where(causal, jnp.broadcast_to(r_row, (L, L)), NEG_BIG)
        g = jnp.maximum(m_prev, jnp.max(rmat, axis=-1, keepdims=True))
        sm = _dot(q, kt) * jnp.exp(rmat - jnp.tile(g, (1, L // LANES)))
        w_inter = jnp.exp(m_prev - g)
        numden = jnp.tile(w_inter, (1, 2)) * _dot(q, cn.astype(BF16)) + _dot(sm.astype(BF16), v1)
        e_neg_m = jnp.exp(-(b_c + g))
        hval = numden[:, 0:M_DH] / jnp.maximum(jnp.abs(numden[:, M_DH:]), e_neg_m)

        g_last = g[L - 1:L, :]
        ws_row = jnp.exp(r_row - jnp.tile(g_last, (1, L // LANES)))
        ktw = (kt.astype(F32) * ws_row).astype(BF16)
        decay = jnp.exp(m_prev - g_last)
        cn_ref[hd] = jnp.tile(decay, (1, 2)) * cn + _dot(ktw, v1)
        m_ref[hd:hd + 1, :] = b_c[L - 1:L, :] + g_last

        hm = jax.nn.sigmoid(o_ref[0, :, sl].astype(F32)) * hval
        mu = jnp.mean(hm, axis=-1, keepdims=True)
        dv = hm - mu
        var = jnp.mean(dv * dv, axis=-1, keepdims=True)
        y = dv * lax.rsqrt(var + EPS) * lng_ref[:, sl]
        y_ref[0, :, sl] = (y * _silu(z_ref[0, :, sl].astype(F32))).astype(BF16)


def _mlstm(u, kt, gc, gtm, ln_g):
    b, s, _ = u.shape
    L = M_CHUNK
    col = lambda g: pl.BlockSpec((1, L, GROUP_W), lambda bi, c, g=g: (bi, c, g))
    return pl.pallas_call(
        _mlstm_kernel,
        grid=(b, s // L),
        in_specs=[
            col(U_MQ),
            pl.BlockSpec((1, M_W, L), lambda bi, c: (bi, 0, c)),
            col(U_MV), col(U_MO), col(U_MZ),
            pl.BlockSpec((1, L, LANES), lambda bi, c: (bi, c, 0)),
            pl.BlockSpec((1, 2 * M_HEADS, L), lambda bi, c: (bi, 0, c)),
            pl.BlockSpec((1, M_W), lambda bi, c: (0, 0)),
        ],
        out_specs=pl.BlockSpec((1, L, M_W), lambda bi, c: (bi, c, 0)),
        out_shape=jax.ShapeDtypeStruct((b, s, M_W), BF16),
        scratch_shapes=[
            pltpu.VMEM((M_HEADS, M_DH, 2 * M_DH), F32),
            pltpu.VMEM((SUBLANES, LANES), F32),
        ],
        compiler_params=pltpu.CompilerParams(
            dimension_semantics=("arbitrary", "arbitrary"), vmem_limit_bytes=VMEM_LIMIT),
        name="mlstm",
    )(u, kt, u, u, u, gc, gtm, ln_g)


def _fox_kernel(q_ref, k_ref, v_ref, z_ref, ft_ref, y_ref, s_ref, mb_ref, al_ref, m_all, accl_all):
    T = FOX_TQ
    nq = q_ref.shape[1] // T
    n_main = nq * (nq - 1) // 2
    U = FOX_UNROLL
    assert U % 2 == 0 and nq % U == 0 and n_main % U == 0 and n_main >= 2 * U
    first_head = lax.broadcasted_iota(jnp.int32, (T, LANES), 1) < F_DH

    def block(ref, blk):
        return ref[0, pl.ds(pl.multiple_of(blk * T, T), T), :]

    def f_rows(blk):
        return ft_ref[0, 0, :, pl.ds(pl.multiple_of(blk * T, T), T)]

    def scores(i, jk, buf, diag):
        q = block(q_ref, i)
        zero = jnp.zeros_like(q)
        q_heads = (jnp.where(first_head, q, zero), jnp.where(first_head, zero, q))
        k = block(k_ref, jk)
        bias = f_rows(i)[:, 0:1] - f_rows(jk)
        for hd in range(2):
            s = _dot_nt(q_heads[hd], k) + bias[hd:hd + 1, :]
            if diag:
                rows = lax.broadcasted_iota(jnp.int32, (T, T), 0)
                cols = lax.broadcasted_iota(jnp.int32, (T, T), 1)
                s = jnp.where(cols <= rows, s, NEG_BIG)
            row_max = jnp.max(s, axis=-1, keepdims=True)
            if diag:
                m_new = jnp.broadcast_to(row_max, (T, LANES))
            else:
                m_old = m_all[i, hd]
                m_new = jnp.maximum(m_old, row_max)
                al_ref[buf, hd] = jnp.exp2(m_old - m_new)
            m_all[i, hd] = m_new
            mb_ref[buf, hd] = m_new
            s_ref[buf, hd] = s

    def accumulate(i, jk, buf, diag):
        v = block(v_ref, jk)
        one = jnp.ones_like(v)
        v_heads = (jnp.where(first_head, v, one), jnp.where(first_head, one, v))
        for hd in range(2):
            p = jnp.exp2(s_ref[buf, hd] - jnp.tile(mb_ref[buf, hd], (1, T // LANES)))
            pv = _dot(p.astype(BF16), v_heads[hd])
            if diag:
                accl_all[i, hd] = pv
            else:
                accl_all[i, hd] = al_ref[buf, hd] * accl_all[i, hd] + pv

    def advance(i, jk):
        wrap = jk + 1 >= i
        return jnp.where(wrap, i + 1, i), jnp.where(wrap, 0, jk + 1)

    scores(0, 0, 0, True)

    def diag_body(jj, carry):
        t0 = U * jj
        for u in range(U):
            scores(t0 + u + 1, t0 + u + 1, (u + 1) % 2, True)
            accumulate(t0 + u, t0 + u, u % 2, True)
        return carry

    lax.fori_loop(0, nq // U - 1, diag_body, 0)
    for u in range(U):
        t = nq - U + u
        if u < U - 1:
            scores(t + 1, t + 1, (u + 1) % 2, True)
        else:
            scores(1, 0, 0, False)
        accumulate(t, t, u % 2, True)

    def main_body(_, cur):
        for u in range(U):
            nxt = advance(*cur)
            scores(*nxt, (u + 1) % 2, False)
            accumulate(*cur, u % 2, False)
            cur = nxt
        return cur

    cur = lax.fori_loop(0, n_main // U - 1, main_body, (jnp.int32(1), jnp.int32(0)))
    for u in range(U):
        nxt = advance(*cur)
        if u < U - 1:
            scores(*nxt, (u + 1) % 2, False)
        accumulate(*cur, u % 2, False)
        cur = nxt

    def finish(i, carry):
        outs = []
        for hd in range(2):
            a = accl_all[i, hd]
            outs.append(a * pltpu.roll(1.0 / a, F_DH, axis=1))
        out = jnp.where(first_head, outs[0], outs[1])
        rows = pl.ds(pl.multiple_of(i * T, T), T)
        y_ref[0, rows, :] = (out * _silu(z_ref[0, rows, :].astype(F32))).astype(BF16)
        return carry

    lax.fori_loop(0, nq, finish, 0)


def _fox(u, ft):
    b, s, _ = u.shape
    T = FOX_TQ
    nq = s // T
    blocks_per_group = GROUP_W // LANES
    qb, kb, vb, zb = (slot * blocks_per_group for slot in (U_FQ, U_FK, U_FV, U_FZ))
    seq = lambda first: pl.BlockSpec((1, s, LANES), lambda bi, p: (bi, 0, first + p))
    return pl.pallas_call(
        _fox_kernel,
        grid=(b, F_HEADS // 2),
        in_specs=[
            seq(qb), seq(kb), seq(vb), seq(zb),
            pl.BlockSpec((1, 1, 2, s), lambda bi, p: (bi, p, 0, 0)),
        ],
        out_specs=seq(0),
        out_shape=jax.ShapeDtypeStruct((b, s, F_W), BF16),
        scratch_shapes=[
            pltpu.VMEM((2, 2, T, T), F32),
            pltpu.VMEM((2, 2, T, LANES), F32),
            pltpu.VMEM((2, 2, T, LANES), F32),
            pltpu.VMEM((nq, 2, T, LANES), F32),
            pltpu.VMEM((nq, 2, T, LANES), F32),
        ],
        compiler_params=pltpu.CompilerParams(
            dimension_semantics=("arbitrary", "arbitrary"), vmem_limit_bytes=VMEM_LIMIT),
        name="fox",
    )(u, u, u, u, ft)


def _out_kernel(x_ref, mod_ref, ym_ref, yf_ref, w_ref, o_ref):
    y = _dot(ym_ref[0], w_ref[0:M_W, :]) + _dot(yf_ref[0], w_ref[M_W:, :])
    o_ref[0] = x_ref[0] + mod_ref[0, 2:3, :] * y


def _out(x, mod3, ym, yf, w_out):
    b, s, d = x.shape
    ts = PROJ_ROWS
    return pl.pallas_call(
        _out_kernel,
        grid=(b, s // ts),
        in_specs=[
            pl.BlockSpec((1, ts, d), lambda bi, j: (bi, j, 0)),
            pl.BlockSpec((1, 3, d), lambda bi, j: (bi, 0, 0)),
            pl.BlockSpec((1, ts, M_W), lambda bi, j: (bi, j, 0)),
            pl.BlockSpec((1, ts, F_W), lambda bi, j: (bi, j, 0)),
            pl.BlockSpec((M_W + F_W, d), lambda bi, j: (0, 0)),
        ],
        out_specs=pl.BlockSpec((1, ts, d), lambda bi, j: (bi, j, 0)),
        out_shape=jax.ShapeDtypeStruct((b, s, d), x.dtype),
        compiler_params=pltpu.CompilerParams(
            dimension_semantics=("arbitrary", "arbitrary"), vmem_limit_bytes=VMEM_LIMIT),
        name="out_proj",
    )(x, mod3, ym, yf, w_out)


def _layer(x, c_pad, norm_g, w_ada, b_ada, w_in, conv_w, conv_b, b_igate, b_fgate_m,
           mlstm_norm_g, b_fgate_f, fox_qnorm_g, fox_knorm_g, w_out):
    b, s, d = x.shape
    mod = _adaln(c_pad, w_ada, b_ada[None, :])
    mod3 = mod[:b].reshape(b, 3, d)

    n_m = 5 * M_W
    n_f = 4 * F_W
    w_big = jnp.concatenate(
        [w_in[:, :n_m], w_in[:, n_m + 2 * M_HEADS:n_m + 2 * M_HEADS + n_f]], axis=1).astype(BF16)
    w_gate = jnp.concatenate(
        [w_in[:, n_m:n_m + 2 * M_HEADS], w_in[:, n_m + 2 * M_HEADS + n_f:]], axis=1)
    n_gate = w_gate.shape[1]
    w_gate = jnp.pad(w_gate, ((0, 0), (0, LANES - n_gate))).astype(BF16)
    gate_b = jnp.pad(jnp.concatenate([b_igate, b_fgate_m, b_fgate_f]), (0, LANES - n_gate))[None, :]
    qg = jnp.tile(fox_qnorm_g, F_HEADS)[None, :] * (LOG2E / math.sqrt(F_DH))
    kg = jnp.tile(fox_knorm_g, F_HEADS)[None, :]
    head_id = jnp.arange(F_W) // F_DH
    hsum = (head_id[:, None] == head_id[None, :]).astype(BF16)

    u, kt, gc, gtm, ft = _proj(x, mod3, norm_g[None, :], w_big, w_gate, conv_w, conv_b[None, :],
                           gate_b, qg, kg, hsum)
    ym = _mlstm(u, kt, gc, gtm, mlstm_norm_g[None, :])
    yf = _fox(u, ft)
    return _out(x, mod3, ym, yf, w_out.astype(BF16))


def kernel(x, c, norm_g, w_ada, b_ada, w_in, conv_w, conv_b, b_igate, b_fgate_m, mlstm_norm_g,
           b_fgate_f, fox_qnorm_g, fox_knorm_g, w_out):
    depth = norm_g.shape[0]
    b = x.shape[0]
    c_pad = jnp.pad(c, ((0, (-b) % SUBLANES), (0, 0)))
    for l in range(depth):
        x = _layer(x, c_pad, norm_g[l], w_ada[l], b_ada[l], w_in[l], conv_w[l], conv_b[l],
                   b_igate[l], b_fgate_m[l], mlstm_norm_g[l], b_fgate_f[l], fox_qnorm_g[l],
                   fox_knorm_g[l], w_out[l])
    return x
```

```python
import functools
import math

import jax
import jax.numpy as jnp
from jax import lax
from jax.experimental import pallas as pl
from jax.experimental.pallas import tpu as pltpu

F32 = jnp.float32
BF16 = jnp.bfloat16

EPS = 1e-6
M_HEADS = 4
M_DH = 128
F_HEADS = 8
F_DH = 64
CONV_K = 4
M_W = M_HEADS * M_DH
F_W = F_HEADS * F_DH
GROUP_W = 512
W_MQ, W_MK, W_MV, W_MO, W_MZ, W_FQ, W_FK, W_FV, W_FZ = range(9)
U_MQ, U_MV, U_MO, U_MZ, U_FQ, U_FK, U_FV, U_FZ = range(8)
N_SLOTS = 8
LANES = 128
SUBLANES = 8
LOG2E = 1.4426950408889634
NEG_BIG = -1e30

PROJ_ROWS = 512
M_CHUNK = 256
M_STEP_ROWS = 2 * M_CHUNK
OUT_ROWS = 1024
FOX_TQ = 512
FOX_UNROLL = 4
VMEM_LIMIT = 56 * 1024 * 1024


def _dot(a, b):
    return jnp.dot(a, b, preferred_element_type=F32)


def _dot_nt(a, b):
    return lax.dot_general(a, b, (((1,), (1,)), ((), ())), preferred_element_type=F32)


def _silu(x):
    return x * jax.nn.sigmoid(x)


def _log_sigmoid(x):
    return jnp.minimum(x, 0.0) - jnp.log1p(jnp.exp(-jnp.abs(x)))


def _adaln_kernel(c_ref, w_ref, b_ref, o_ref):
    c = c_ref[...]
    w = w_ref[...]
    c_hi = c.astype(BF16)
    c_lo = (c - c_hi.astype(F32)).astype(BF16)
    w_hi = w.astype(BF16)
    w_lo = (w - w_hi.astype(F32)).astype(BF16)
    acc = _dot(c_hi, w_hi) + _dot(c_hi, w_lo) + _dot(c_lo, w_hi)
    o_ref[...] = acc + b_ref[...]


def _adaln(c_pad, w_ada, b_ada):
    rows, d = c_pad.shape
    n = w_ada.shape[1]
    tn = d
    return pl.pallas_call(
        _adaln_kernel,
        grid=(n // tn,),
        in_specs=[
            pl.BlockSpec((rows, d), lambda j: (0, 0)),
            pl.BlockSpec((d, tn), lambda j: (0, j)),
            pl.BlockSpec((1, tn), lambda j: (0, j)),
        ],
        out_specs=pl.BlockSpec((rows, tn), lambda j: (0, j)),
        out_shape=jax.ShapeDtypeStruct((rows, n), F32),
        compiler_params=pltpu.CompilerParams(
            dimension_semantics=("arbitrary",), vmem_limit_bytes=VMEM_LIMIT),
        name="adaln",
    )(c_pad, w_ada, b_ada)


def _seg_cumsum(x, seg_row, n):
    shift = 1
    while shift < n:
        rolled = pltpu.roll(x, shift, axis=0)
        x = x + jnp.where(seg_row >= shift, rolled, 0.0)
        shift *= 2
    return x


def _proj_kernel(x_ref, mod_ref, ng_ref, w_ref, wg_ref, cw_ref, cb_ref, gb_ref, qg_ref, kg_ref,
                 hsum_ref, u_ref, kt_ref, gc_ref, gtm_ref, ft_ref, ext_ref, cum_ref):
    j = pl.program_id(1)
    ts = x_ref.shape[1]

    @pl.when(j == 0)
    def _():
        ext_ref[0:SUBLANES, :] = jnp.zeros((SUBLANES, ext_ref.shape[1]), F32)
        cum_ref[...] = jnp.zeros_like(cum_ref)

    x = x_ref[0]
    ms = jnp.mean(x * x, axis=-1, keepdims=True)
    shift = mod_ref[0, 0:1, :]
    gain = ng_ref[...] * (1.0 + mod_ref[0, 1:2, :])
    h = (x * lax.rsqrt(ms + EPS) * gain + shift).astype(BF16)

    def group(g):
        return _dot(h, w_ref[:, g * GROUP_W:(g + 1) * GROUP_W])

    def store(slot, val):
        u_ref[0, :, slot * GROUP_W:(slot + 1) * GROUP_W] = val.astype(BF16)

    ug = _dot(h, wg_ref[...]) + gb_ref[...]
    lane = lax.broadcasted_iota(jnp.int32, ug.shape, 1)
    row = lax.broadcasted_iota(jnp.int32, ug.shape, 0)
    lf = _log_sigmoid(ug)
    lf = jnp.where(lane >= 2 * M_HEADS, lf * LOG2E, lf)
    seg_row = jnp.where(lane < 2 * M_HEADS, row & (M_CHUNK - 1), row)
    cs = _seg_cumsum(lf, seg_row, ts)
    cs = cs + jnp.where(lane >= 2 * M_HEADS, cum_ref[0:1, :], 0.0)
    cum_ref[0:1, :] = cs[ts - 1:ts, :]
    gc = jnp.where(lane < M_HEADS, ug, cs)
    gc_ref[0] = gc
    gt = gc.T
    gtm_ref[0] = gt[0:2 * M_HEADS, :]
    for p in range(F_HEADS // 2):
        lo = 2 * M_HEADS + 2 * p
        ft_ref[0, p] = gt[lo:lo + 2, :]

    for g, slot, gain_ref in ((W_FQ, U_FQ, qg_ref), (W_FK, U_FK, kg_ref)):
        u = group(g)
        sq = (u * u).astype(BF16)
        half = hsum_ref.shape[0]
        ssq = jnp.concatenate([_dot(sq[:, 0:half], hsum_ref[...]), _dot(sq[:, half:], hsum_ref[...])], axis=1)
        store(slot, u * lax.rsqrt(ssq * (1.0 / F_DH) + EPS) * gain_ref[...])

    ext_ref[SUBLANES:, 0:GROUP_W] = group(W_MQ)
    ext_ref[SUBLANES:, GROUP_W:] = group(W_MK)
    conv = cb_ref[...] + cw_ref[0:1, :] * ext_ref[SUBLANES - 3:SUBLANES - 3 + ts, :]
    for t in range(1, CONV_K):
        lo = SUBLANES - 3 + t
        conv = conv + cw_ref[t:t + 1, :] * ext_ref[lo:lo + ts, :]
    ext_ref[0:SUBLANES, :] = ext_ref[ts:ts + SUBLANES, :]
    qk = _silu(conv)
    store(U_MQ, qk[:, 0:GROUP_W])
    kt_ref[0] = (qk[:, GROUP_W:] * (1.0 / math.sqrt(M_DH))).T.astype(BF16)

    for g, slot in ((W_MV, U_MV), (W_MO, U_MO), (W_MZ, U_MZ), (W_FV, U_FV), (W_FZ, U_FZ)):
        store(slot, group(g))


def _proj(x, mod3, norm_g, w_big, w_gate, conv_w, conv_b, gate_b, qg, kg, hsum):
    b, s, d = x.shape
    ts = PROJ_ROWS
    n_big = w_big.shape[1]
    const = lambda *shape: pl.BlockSpec(shape, lambda bi, j: (0,) * len(shape))
    return pl.pallas_call(
        _proj_kernel,
        grid=(b, s // ts),
        in_specs=[
            pl.BlockSpec((1, ts, d), lambda bi, j: (bi, j, 0)),
            pl.BlockSpec((1, 3, d), lambda bi, j: (bi, 0, 0)),
            const(1, d),
            const(d, n_big),
            const(d, LANES),
            const(CONV_K, 2 * M_W),
            const(1, 2 * M_W),
            const(1, LANES),
            const(1, F_W),
            const(1, F_W),
            const(F_W // 2, F_W // 2),
        ],
        out_specs=[
            pl.BlockSpec((1, ts, N_SLOTS * GROUP_W), lambda bi, j: (bi, j, 0)),
            pl.BlockSpec((1, M_W, ts), lambda bi, j: (bi, 0, j)),
            pl.BlockSpec((1, ts, LANES), lambda bi, j: (bi, j, 0)),
            pl.BlockSpec((1, 2 * M_HEADS, ts), lambda bi, j: (bi, 0, j)),
            pl.BlockSpec((1, F_HEADS // 2, 2, ts), lambda bi, j: (bi, 0, 0, j)),
        ],
        out_shape=[
            jax.ShapeDtypeStruct((b, s, N_SLOTS * GROUP_W), BF16),
            jax.ShapeDtypeStruct((b, M_W, s), BF16),
            jax.ShapeDtypeStruct((b, s, LANES), F32),
            jax.ShapeDtypeStruct((b, 2 * M_HEADS, s), F32),
            jax.ShapeDtypeStruct((b, F_HEADS // 2, 2, s), F32),
        ],
        scratch_shapes=[
            pltpu.VMEM((ts + SUBLANES, 2 * M_W), F32),
            pltpu.VMEM((SUBLANES, LANES), F32),
        ],
        compiler_params=pltpu.CompilerParams(
            dimension_semantics=("arbitrary", "arbitrary"), vmem_limit_bytes=VMEM_LIMIT),
        name="proj",
    )(x, mod3, norm_g, w_big, w_gate, conv_w, conv_b, gate_b, qg, kg, hsum)


def _mlstm_kernel(q_ref, kt_ref, v_ref, o_ref, z_ref, gc_ref, gt_ref, lng_ref, y_ref, cn_ref, m_ref):
    c = pl.program_id(1)
    L = M_CHUNK

    @pl.when(c == 0)
    def _():
        cn_ref[...] = jnp.zeros_like(cn_ref)
        m_ref[...] = jnp.zeros_like(m_ref)

    rows = lax.broadcasted_iota(jnp.int32, (L, L), 0)
    cols = lax.broadcasted_iota(jnp.int32, (L, L), 1)
    causal = cols <= rows
    ones_blk = jnp.ones((L, M_DH), BF16)

    for ck, hd in [(ck, hd) for ck in range(q_ref.shape[1] // L) for hd in range(M_HEADS)]:
        t = slice(ck * L, (ck + 1) * L)
        sl = slice(hd * M_DH, (hd + 1) * M_DH)
        q = q_ref[0, t, sl]
        kt = kt_ref[0, sl, t]
        v1 = jnp.concatenate([v_ref[0, t, sl], ones_blk], axis=1)
        b_c = jnp.broadcast_to(gc_ref[0, t, M_HEADS + hd:M_HEADS + hd + 1], (L, LANES))
        r_row = gt_ref[0, hd:hd + 1, t] - gt_ref[0, M_HEADS + hd:M_HEADS + hd + 1, t]
        m_prev = m_ref[hd:hd + 1, :]
        cn = cn_ref[hd]

        rmat = jnp.where(causal, jnp.broadcast_to(r_row, (L, L)), NEG_BIG)
        g = jnp.maximum(m_prev, jnp.max(rmat, axis=-1, keepdims=True))
        sm = _dot(q, kt) * jnp.exp(rmat - jnp.tile(g, (1, L // LANES)))
        w_inter = jnp.exp(m_prev - g)
        numden = jnp.tile(w_inter, (1, 2)) * _dot(q, cn.astype(BF16)) + _dot(sm.astype(BF16), v1)
        e_neg_m = jnp.exp(-(b_c + g))
        hval = numden[:, 0:M_DH] / jnp.maximum(jnp.abs(numden[:, M_DH:]), e_neg_m)

        g_last = g[L - 1:L, :]
        ws_row = jnp.exp(r_row - jnp.tile(g_last, (1, L // LANES)))
        ktw = (kt.astype(F32) * ws_row).astype(BF16)
        decay = jnp.exp(m_prev - g_last)
        cn_ref[hd] = jnp.tile(decay, (1, 2)) * cn + _dot(ktw, v1)
        m_ref[hd:hd + 1, :] = b_c[L - 1:L, :] + g_last

        hm = jax.nn.sigmoid(o_ref[0, t, sl].astype(F32)) * hval
        mu = jnp.mean(hm, axis=-1, keepdims=True)
        dv = hm - mu
        var = jnp.mean(dv * dv, axis=-1, keepdims=True)
        y = dv * lax.rsqrt(var + EPS) * lng_ref[:, sl]
        y_ref[0, t, sl] = (y * _silu(z_ref[0, t, sl].astype(F32))).astype(BF16)


def _mlstm(u, kt, gc, gtm, ln_g):
    b, s, _ = u.shape
    L = M_STEP_ROWS
    col = lambda g: pl.BlockSpec((1, L, GROUP_W), lambda bi, c, g=g: (bi, c, g))
    return pl.pallas_call(
        _mlstm_kernel,
        grid=(b, s // L),
        in_specs=[
            col(U_MQ),
            pl.BlockSpec((1, M_W, L), lambda bi, c: (bi, 0, c)),
            col(U_MV), col(U_MO), col(U_MZ),
            pl.BlockSpec((1, L, LANES), lambda bi, c: (bi, c, 0)),
            pl.BlockSpec((1, 2 * M_HEADS, L), lambda bi, c: (bi, 0, c)),
            pl.BlockSpec((1, M_W), lambda bi, c: (0, 0)),
        ],
        out_specs=pl.BlockSpec((1, L, M_W), lambda bi, c: (bi, c, 0)),
        out_shape=jax.ShapeDtypeStruct((b, s, M_W), BF16),
        scratch_shapes=[
            pltpu.VMEM((M_HEADS, M_DH, 2 * M_DH), F32),
            pltpu.VMEM((SUBLANES, LANES), F32),
        ],
        compiler_params=pltpu.CompilerParams(
            dimension_semantics=("arbitrary", "arbitrary"), vmem_limit_bytes=VMEM_LIMIT),
        name="mlstm",
    )(u, kt, u, u, u, gc, gtm, ln_g)


def _fox_kernel(q_ref, k_ref, v_ref, z_ref, ft_ref, y_ref, s_ref, mb_ref, al_ref, m_all, accl_all):
    T = FOX_TQ
    nq = q_ref.shape[1] // T
    n_main = nq * (nq - 1) // 2
    U = FOX_UNROLL
    assert U % 2 == 0 and nq % U == 0 and n_main % U == 0 and n_main >= 2 * U
    first_head = lax.broadcasted_iota(jnp.int32, (T, LANES), 1) < F_DH

    def block(ref, blk):
        return ref[0, pl.ds(pl.multiple_of(blk * T, T), T), :]

    def f_rows(blk):
        return ft_ref[0, 0, :, pl.ds(pl.multiple_of(blk * T, T), T)]

    def scores(i, jk, buf, diag):
        q = block(q_ref, i)
        zero = jnp.zeros_like(q)
        q_heads = (jnp.where(first_head, q, zero), jnp.where(first_head, zero, q))
        k = block(k_ref, jk)
        bias = f_rows(i)[:, 0:1] - f_rows(jk)
        for hd in range(2):
            s = _dot_nt(q_heads[hd], k) + bias[hd:hd + 1, :]
            if diag:
                rows = lax.broadcasted_iota(jnp.int32, (T, T), 0)
                cols = lax.broadcasted_iota(jnp.int32, (T, T), 1)
                s = jnp.where(cols <= rows, s, NEG_BIG)
            row_max = jnp.max(s, axis=-1, keepdims=True)
            if diag:
                m_new = jnp.broadcast_to(row_max, (T, LANES))
            else:
                m_old = m_all[i, hd]
                m_new = jnp.maximum(m_old, row_max)
                al_ref[buf, hd] = jnp.exp2(m_old - m_new)
            m_all[i, hd] = m_new
            mb_ref[buf, hd] = m_new
            s_ref[buf, hd] = s

    def accumulate(i, jk, buf, diag):
        v = block(v_ref, jk)
        one = jnp.ones_like(v)
        v_heads = (jnp.where(first_head, v, one), jnp.where(first_head, one, v))
        for hd in range(2):
            p = jnp.exp2(s_ref[buf, hd] - jnp.tile(mb_ref[buf, hd], (1, T // LANES)))
            pv = _dot(p.astype(BF16), v_heads[hd])
            if diag:
                accl_all[i, hd] = pv
            else:
                accl_all[i, hd] = al_ref[buf, hd] * accl_all[i, hd] + pv

    def advance(i, jk):
        wrap = jk + 1 >= i
        return jnp.where(wrap, i + 1, i), jnp.where(wrap, 0, jk + 1)

    scores(0, 0, 0, True)

    def diag_body(jj, carry):
        t0 = U * jj
        for u in range(U):
            scores(t0 + u + 1, t0 + u + 1, (u + 1) % 2, True)
            accumulate(t0 + u, t0 + u, u % 2, True)
        return carry

    lax.fori_loop(0, nq // U - 1, diag_body, 0)
    for u in range(U):
        t = nq - U + u
        if u < U - 1:
            scores(t + 1, t + 1, (u + 1) % 2, True)
        else:
            scores(1, 0, 0, False)
        accumulate(t, t, u % 2, True)

    def main_body(_, cur):
        for u in range(U):
            nxt = advance(*cur)
            scores(*nxt, (u + 1) % 2, False)
            accumulate(*cur, u % 2, False)
            cur = nxt
        return cur

    cur = lax.fori_loop(0, n_main // U - 1, main_body, (jnp.int32(1), jnp.int32(0)))
    for u in range(U):
        nxt = advance(*cur)
        if u < U - 1:
            scores(*nxt, (u + 1) % 2, False)
        accumulate(*cur, u % 2, False)
        cur = nxt

    def finish(i, carry):
        outs = []
        for hd in range(2):
            a = accl_all[i, hd]
            outs.append(a * pltpu.roll(1.0 / a, F_DH, axis=1))
        out = jnp.where(first_head, outs[0], outs[1])
        rows = pl.ds(pl.multiple_of(i * T, T), T)
        y_ref[0, rows, :] = (out * _silu(z_ref[0, rows, :].astype(F32))).astype(BF16)
        return carry

    lax.fori_loop(0, nq, finish, 0)


def _fox(u, ft):
    b, s, _ = u.shape
    T = FOX_TQ
    nq = s // T
    blocks_per_group = GROUP_W // LANES
    qb, kb, vb, zb = (slot * blocks_per_group for slot in (U_FQ, U_FK, U_FV, U_FZ))
    seq = lambda first: pl.BlockSpec((1, s, LANES), lambda bi, p: (bi, 0, first + p))
    return pl.pallas_call(
        _fox_kernel,
        grid=(b, F_HEADS // 2),
        in_specs=[
            seq(qb), seq(kb), seq(vb), seq(zb),
            pl.BlockSpec((1, 1, 2, s), lambda bi, p: (bi, p, 0, 0)),
        ],
        out_specs=seq(0),
        out_shape=jax.ShapeDtypeStruct((b, s, F_W), BF16),
        scratch_shapes=[
            pltpu.VMEM((2, 2, T, T), F32),
            pltpu.VMEM((2, 2, T, LANES), F32),
            pltpu.VMEM((2, 2, T, LANES), F32),
            pltpu.VMEM((nq, 2, T, LANES), F32),
            pltpu.VMEM((nq, 2, T, LANES), F32),
        ],
        compiler_params=pltpu.CompilerParams(
            dimension_semantics=("arbitrary", "arbitrary"), vmem_limit_bytes=VMEM_LIMIT),
        name="fox",
    )(u, u, u, u, ft)


def _out_kernel(x_ref, mod_ref, ym_ref, yf_ref, w_ref, o_ref):
    y = _dot(ym_ref[0], w_ref[0:M_W, :]) + _dot(yf_ref[0], w_ref[M_W:, :])
    o_ref[0] = x_ref[0] + mod_ref[0, 2:3, :] * y


def _out(x, mod3, ym, yf, w_out):
    b, s, d = x.shape
    ts = OUT_ROWS
    return pl.pallas_call(
        _out_kernel,
        grid=(b, s // ts),
        in_specs=[
            pl.BlockSpec((1, ts, d), lambda bi, j: (bi, j, 0)),
            pl.BlockSpec((1, 3, d), lambda bi, j: (bi, 0, 0)),
            pl.BlockSpec((1, ts, M_W), lambda bi, j: (bi, j, 0)),
            pl.BlockSpec((1, ts, F_W), lambda bi, j: (bi, j, 0)),
            pl.BlockSpec((M_W + F_W, d), lambda bi, j: (0, 0)),
        ],
        out_specs=pl.BlockSpec((1, ts, d), lambda bi, j: (bi, j, 0)),
        out_shape=jax.ShapeDtypeStruct((b, s, d), x.dtype),
        compiler_params=pltpu.CompilerParams(
            dimension_semantics=("arbitrary", "arbitrary"), vmem_limit_bytes=VMEM_LIMIT),
        name="out_proj",
    )(x, mod3, ym, yf, w_out)


def _layer(x, c_pad, norm_g, w_ada, b_ada, w_in, conv_w, conv_b, b_igate, b_fgate_m,
           mlstm_norm_g, b_fgate_f, fox_qnorm_g, fox_knorm_g, w_out):
    b, s, d = x.shape
    mod = _adaln(c_pad, w_ada, b_ada[None, :])
    mod3 = mod[:b].reshape(b, 3, d)

    n_m = 5 * M_W
    n_f = 4 * F_W
    w_big = jnp.concatenate(
        [w_in[:, :n_m], w_in[:, n_m + 2 * M_HEADS:n_m + 2 * M_HEADS + n_f]], axis=1).astype(BF16)
    w_gate = jnp.concatenate(
        [w_in[:, n_m:n_m + 2 * M_HEADS], w_in[:, n_m + 2 * M_HEADS + n_f:]], axis=1)
    n_gate = w_gate.shape[1]
    w_gate = jnp.pad(w_gate, ((0, 0), (0, LANES - n_gate))).astype(BF16)
    gate_b = jnp.pad(jnp.concatenate([b_igate, b_fgate_m, b_fgate_f]), (0, LANES - n_gate))[None, :]
    qg = jnp.tile(fox_qnorm_g, F_HEADS)[None, :] * (LOG2E / math.sqrt(F_DH))
    kg = jnp.tile(fox_knorm_g, F_HEADS)[None, :]
    head_id = jnp.arange(F_W // 2) // F_DH
    hsum = (head_id[:, None] == head_id[None, :]).astype(BF16)

    u, kt, gc, gtm, ft = _proj(x, mod3, norm_g[None, :], w_big, w_gate, conv_w, conv_b[None, :],
                           gate_b, qg, kg, hsum)
    ym = _mlstm(u, kt, gc, gtm, mlstm_norm_g[None, :])
    yf = _fox(u, ft)
    return _out(x, mod3, ym, yf, w_out.astype(BF16))


def kernel(x, c, norm_g, w_ada, b_ada, w_in, conv_w, conv_b, b_igate, b_fgate_m, mlstm_norm_g,
           b_fgate_f, fox_qnorm_g, fox_knorm_g, w_out):
    depth = norm_g.shape[0]
    b = x.shape[0]
    c_pad = jnp.pad(c, ((0, (-b) % SUBLANES), (0, 0)))
    for l in range(depth):
        x = _layer(x, c_pad, norm_g[l], w_ada[l], b_ada[l], w_in[l], conv_w[l], conv_b[l],
                   b_igate[l], b_fgate_m[l], mlstm_norm_g[l], b_fgate_f[l], fox_qnorm_g[l],
                   fox_knorm_g[l], w_out[l])
    return x
```

```python
import functools
import math

import jax
import jax.numpy as jnp
from jax import lax
from jax.experimental import pallas as pl
from jax.experimental.pallas import tpu as pltpu

F32 = jnp.float32
BF16 = jnp.bfloat16

EPS = 1e-6
M_HEADS = 4
M_DH = 128
F_HEADS = 8
F_DH = 64
CONV_K = 4
M_W = M_HEADS * M_DH
F_W = F_HEADS * F_DH
GROUP_W = 512
W_MQ, W_MK, W_MV, W_MO, W_MZ, W_FQ, W_FK, W_FV, W_FZ = range(9)
U_MQ, U_MV, U_MO, U_MZ, U_FQ, U_FK, U_FV, U_FZ = range(8)
N_SLOTS = 8
LANES = 128
SUBLANES = 8
LOG2E = 1.4426950408889634
NEG_BIG = -1e30

PROJ_ROWS = 512
M_CHUNK = 256
M_STEP_ROWS = 2 * M_CHUNK
OUT_ROWS = 1024
FOX_TQ = 512
FOX_UNROLL = 8
VMEM_LIMIT = 56 * 1024 * 1024


def _dot(a, b):
    return jnp.dot(a, b, preferred_element_type=F32)


def _dot_nt(a, b):
    return lax.dot_general(a, b, (((1,), (1,)), ((), ())), preferred_element_type=F32)


def _silu(x):
    return x * jax.nn.sigmoid(x)


def _log_sigmoid(x):
    return jnp.minimum(x, 0.0) - jnp.log1p(jnp.exp(-jnp.abs(x)))


def _adaln_kernel(c_ref, w_ref, b_ref, o_ref):
    c = c_ref[...]
    w = w_ref[...]
    c_hi = c.astype(BF16)
    c_lo = (c - c_hi.astype(F32)).astype(BF16)
    w_hi = w.astype(BF16)
    w_lo = (w - w_hi.astype(F32)).astype(BF16)
    acc = _dot(c_hi, w_hi) + _dot(c_hi, w_lo) + _dot(c_lo, w_hi)
    o_ref[...] = acc + b_ref[...]


def _adaln(c_pad, w_ada, b_ada):
    rows, d = c_pad.shape
    n = w_ada.shape[1]
    tn = d
    return pl.pallas_call(
        _adaln_kernel,
        grid=(n // tn,),
        in_specs=[
            pl.BlockSpec((rows, d), lambda j: (0, 0)),
            pl.BlockSpec((d, tn), lambda j: (0, j)),
            pl.BlockSpec((1, tn), lambda j: (0, j)),
        ],
        out_specs=pl.BlockSpec((rows, tn), lambda j: (0, j)),
        out_shape=jax.ShapeDtypeStruct((rows, n), F32),
        compiler_params=pltpu.CompilerParams(
            dimension_semantics=("arbitrary",), vmem_limit_bytes=VMEM_LIMIT),
        name="adaln",
    )(c_pad, w_ada, b_ada)


def _seg_cumsum(x, seg_row, n):
    shift = 1
    while shift < n:
        rolled = pltpu.roll(x, shift, axis=0)
        x = x + jnp.where(seg_row >= shift, rolled, 0.0)
        shift *= 2
    return x


def _proj_kernel(x_ref, mod_ref, ng_ref, w_ref, wg_ref, cw_ref, cb_ref, gb_ref, qg_ref, kg_ref,
                 hsum_ref, u_ref, kt_ref, gc_ref, gtm_ref, ft_ref, ext_ref, cum_ref):
    j = pl.program_id(1)
    ts = x_ref.shape[1]

    @pl.when(j == 0)
    def _():
        ext_ref[0:SUBLANES, :] = jnp.zeros((SUBLANES, ext_ref.shape[1]), F32)
        cum_ref[...] = jnp.zeros_like(cum_ref)

    x = x_ref[0]
    ms = jnp.mean(x * x, axis=-1, keepdims=True)
    shift = mod_ref[0, 0:1, :]
    gain = ng_ref[...] * (1.0 + mod_ref[0, 1:2, :])
    h = (x * lax.rsqrt(ms + EPS) * gain + shift).astype(BF16)

    def group(g):
        return _dot(h, w_ref[:, g * GROUP_W:(g + 1) * GROUP_W])

    def store(slot, val):
        u_ref[0, :, slot * GROUP_W:(slot + 1) * GROUP_W] = val.astype(BF16)

    ug = _dot(h, wg_ref[...]) + gb_ref[...]
    lane = lax.broadcasted_iota(jnp.int32, ug.shape, 1)
    row = lax.broadcasted_iota(jnp.int32, ug.shape, 0)
    lf = _log_sigmoid(ug) * LOG2E
    seg_row = jnp.where(lane < 2 * M_HEADS, row & (M_CHUNK - 1), row)
    cs = _seg_cumsum(lf, seg_row, ts)
    cs = cs + jnp.where(lane >= 2 * M_HEADS, cum_ref[0:1, :], 0.0)
    cum_ref[0:1, :] = cs[ts - 1:ts, :]
    gc = jnp.where(lane < M_HEADS, ug * LOG2E, cs)
    gc_ref[0] = gc
    gt = gc.T
    gtm_ref[0] = gt[0:2 * M_HEADS, :]
    for p in range(F_HEADS // 2):
        lo = 2 * M_HEADS + 2 * p
        ft_ref[0, p] = gt[lo:lo + 2, :]

    for g, slot, gain_ref in ((W_FQ, U_FQ, qg_ref), (W_FK, U_FK, kg_ref)):
        u = group(g)
        sq = (u * u).astype(BF16)
        half = hsum_ref.shape[0]
        ssq = jnp.concatenate([_dot(sq[:, 0:half], hsum_ref[...]), _dot(sq[:, half:], hsum_ref[...])], axis=1)
        store(slot, u * lax.rsqrt(ssq * (1.0 / F_DH) + EPS) * gain_ref[...])

    ext_ref[SUBLANES:, 0:GROUP_W] = group(W_MQ)
    ext_ref[SUBLANES:, GROUP_W:] = group(W_MK)
    conv = cb_ref[...] + cw_ref[0:1, :] * ext_ref[SUBLANES - 3:SUBLANES - 3 + ts, :]
    for t in range(1, CONV_K):
        lo = SUBLANES - 3 + t
        conv = conv + cw_ref[t:t + 1, :] * ext_ref[lo:lo + ts, :]
    ext_ref[0:SUBLANES, :] = ext_ref[ts:ts + SUBLANES, :]
    qk = _silu(conv)
    store(U_MQ, qk[:, 0:GROUP_W])
    kt_ref[0] = (qk[:, GROUP_W:] * (1.0 / math.sqrt(M_DH))).T.astype(BF16)

    for g, slot in ((W_MV, U_MV), (W_MO, U_MO), (W_MZ, U_MZ), (W_FV, U_FV), (W_FZ, U_FZ)):
        store(slot, group(g))


def _proj(x, mod3, norm_g, w_big, w_gate, conv_w, conv_b, gate_b, qg, kg, hsum):
    b, s, d = x.shape
    ts = PROJ_ROWS
    n_big = w_big.shape[1]
    const = lambda *shape: pl.BlockSpec(shape, lambda bi, j: (0,) * len(shape))
    return pl.pallas_call(
        _proj_kernel,
        grid=(b, s // ts),
        in_specs=[
            pl.BlockSpec((1, ts, d), lambda bi, j: (bi, j, 0)),
            pl.BlockSpec((1, 3, d), lambda bi, j: (bi, 0, 0)),
            const(1, d),
            const(d, n_big),
            const(d, LANES),
            const(CONV_K, 2 * M_W),
            const(1, 2 * M_W),
            const(1, LANES),
            const(1, F_W),
            const(1, F_W),
            const(F_W // 2, F_W // 2),
        ],
        out_specs=[
            pl.BlockSpec((1, ts, N_SLOTS * GROUP_W), lambda bi, j: (bi, j, 0)),
            pl.BlockSpec((1, M_W, ts), lambda bi, j: (bi, 0, j)),
            pl.BlockSpec((1, ts, LANES), lambda bi, j: (bi, j, 0)),
            pl.BlockSpec((1, 2 * M_HEADS, ts), lambda bi, j: (bi, 0, j)),
            pl.BlockSpec((1, F_HEADS // 2, 2, ts), lambda bi, j: (bi, 0, 0, j)),
        ],
        out_shape=[
            jax.ShapeDtypeStruct((b, s, N_SLOTS * GROUP_W), BF16),
            jax.ShapeDtypeStruct((b, M_W, s), BF16),
            jax.ShapeDtypeStruct((b, s, LANES), F32),
            jax.ShapeDtypeStruct((b, 2 * M_HEADS, s), F32),
            jax.ShapeDtypeStruct((b, F_HEADS // 2, 2, s), F32),
        ],
        scratch_shapes=[
            pltpu.VMEM((ts + SUBLANES, 2 * M_W), F32),
            pltpu.VMEM((SUBLANES, LANES), F32),
        ],
        compiler_params=pltpu.CompilerParams(
            dimension_semantics=("arbitrary", "arbitrary"), vmem_limit_bytes=VMEM_LIMIT),
        name="proj",
    )(x, mod3, norm_g, w_big, w_gate, conv_w, conv_b, gate_b, qg, kg, hsum)


def _mlstm_kernel(q_ref, kt_ref, v_ref, o_ref, z_ref, gc_ref, gt_ref, lng_ref, y_ref, cn_ref, m_ref):
    c = pl.program_id(1)
    L = M_CHUNK

    @pl.when(c == 0)
    def _():
        cn_ref[...] = jnp.zeros_like(cn_ref)
        m_ref[...] = jnp.zeros_like(m_ref)

    rows = lax.broadcasted_iota(jnp.int32, (L, L), 0)
    cols = lax.broadcasted_iota(jnp.int32, (L, L), 1)
    causal = cols <= rows
    ones_blk = jnp.ones((L, M_DH), BF16)

    for ck, hd in [(ck, hd) for ck in range(q_ref.shape[1] // L) for hd in range(M_HEADS)]:
        t = slice(ck * L, (ck + 1) * L)
        sl = slice(hd * M_DH, (hd + 1) * M_DH)
        q = q_ref[0, t, sl]
        kt = kt_ref[0, sl, t]
        v1 = jnp.concatenate([v_ref[0, t, sl], ones_blk], axis=1)
        b_c = jnp.broadcast_to(gc_ref[0, t, M_HEADS + hd:M_HEADS + hd + 1], (L, LANES))
        r_row = gt_ref[0, hd:hd + 1, t] - gt_ref[0, M_HEADS + hd:M_HEADS + hd + 1, t]
        m_prev = m_ref[hd:hd + 1, :]
        cn = cn_ref[hd]

        rmat = jnp.where(causal, jnp.broadcast_to(r_row, (L, L)), NEG_BIG)
        g = jnp.maximum(m_prev, jnp.max(rmat, axis=-1, keepdims=True))
        sm = _dot(q, kt) * jnp.exp2(rmat - jnp.tile(g, (1, L // LANES)))
        w_inter = jnp.exp2(m_prev - g)
        numden = jnp.tile(w_inter, (1, 2)) * _dot(q, cn.astype(BF16)) + _dot(sm.astype(BF16), v1)
        e_neg_m = jnp.exp2(-(b_c + g))
        den = jnp.maximum(jnp.abs(numden[:, M_DH:]), e_neg_m)

        g_last = g[L - 1:L, :]
        ws_row = jnp.exp2(r_row - jnp.tile(g_last, (1, L // LANES)))
        ktw = kt * ws_row.astype(BF16)
        decay = jnp.exp2(m_prev - g_last)
        cn_ref[hd] = jnp.tile(decay, (1, 2)) * cn + _dot(ktw, v1)
        m_ref[hd:hd + 1, :] = b_c[L - 1:L, :] + g_last

        hm = numden[:, 0:M_DH] / (den * (1.0 + jnp.exp2(o_ref[0, t, sl].astype(F32) * (-LOG2E))))
        mu = jnp.mean(hm, axis=-1, keepdims=True)
        dv = hm - mu
        var = jnp.mean(dv * dv, axis=-1, keepdims=True)
        y = dv * lax.rsqrt(var + EPS) * lng_ref[:, sl]
        y_ref[0, t, sl] = (y * _silu(z_ref[0, t, sl].astype(F32))).astype(BF16)


def _mlstm(u, kt, gc, gtm, ln_g):
    b, s, _ = u.shape
    L = M_STEP_ROWS
    col = lambda g: pl.BlockSpec((1, L, GROUP_W), lambda bi, c, g=g: (bi, c, g))
    return pl.pallas_call(
        _mlstm_kernel,
        grid=(b, s // L),
        in_specs=[
            col(U_MQ),
            pl.BlockSpec((1, M_W, L), lambda bi, c: (bi, 0, c)),
            col(U_MV), col(U_MO), col(U_MZ),
            pl.BlockSpec((1, L, LANES), lambda bi, c: (bi, c, 0)),
            pl.BlockSpec((1, 2 * M_HEADS, L), lambda bi, c: (bi, 0, c)),
            pl.BlockSpec((1, M_W), lambda bi, c: (0, 0)),
        ],
        out_specs=pl.BlockSpec((1, L, M_W), lambda bi, c: (bi, c, 0)),
        out_shape=jax.ShapeDtypeStruct((b, s, M_W), BF16),
        scratch_shapes=[
            pltpu.VMEM((M_HEADS, M_DH, 2 * M_DH), F32),
            pltpu.VMEM((SUBLANES, LANES), F32),
        ],
        compiler_params=pltpu.CompilerParams(
            dimension_semantics=("arbitrary", "arbitrary"), vmem_limit_bytes=VMEM_LIMIT),
        name="mlstm",
    )(u, kt, u, u, u, gc, gtm, ln_g)


def _fox_kernel(q_ref, k_ref, v_ref, z_ref, ft_ref, y_ref, s_ref, mb_ref, al_ref, m_all, accl_all):
    T = FOX_TQ
    nq = q_ref.shape[1] // T
    n_main = nq * (nq - 1) // 2
    U = FOX_UNROLL
    assert U % 2 == 0 and nq % U == 0 and n_main % U == 0 and n_main >= 2 * U
    first_head = lax.broadcasted_iota(jnp.int32, (T, LANES), 1) < F_DH

    def block(ref, blk):
        return ref[0, pl.ds(pl.multiple_of(blk * T, T), T), :]

    def f_rows(blk):
        return ft_ref[0, 0, :, pl.ds(pl.multiple_of(blk * T, T), T)]

    def scores(i, jk, buf, diag):
        q = block(q_ref, i)
        zero = jnp.zeros_like(q)
        q_heads = (jnp.where(first_head, q, zero), jnp.where(first_head, zero, q))
        k = block(k_ref, jk)
        bias = f_rows(i)[:, 0:1] - f_rows(jk)
        for hd in range(2):
            s = _dot_nt(q_heads[hd], k) + bias[hd:hd + 1, :]
            if diag:
                rows = lax.broadcasted_iota(jnp.int32, (T, T), 0)
                cols = lax.broadcasted_iota(jnp.int32, (T, T), 1)
                s = jnp.where(cols <= rows, s, NEG_BIG)
            row_max = jnp.max(s, axis=-1, keepdims=True)
            if diag:
                m_new = jnp.broadcast_to(row_max, (T, LANES))
            else:
                m_old = m_all[i, hd]
                m_new = jnp.maximum(m_old, row_max)
                al_ref[buf, hd] = jnp.exp2(m_old - m_new)
            m_all[i, hd] = m_new
            mb_ref[buf, hd] = m_new
            s_ref[buf, hd] = s

    def accumulate(i, jk, buf, diag):
        v = block(v_ref, jk)
        one = jnp.ones_like(v)
        v_heads = (jnp.where(first_head, v, one), jnp.where(first_head, one, v))
        for hd in range(2):
            p = jnp.exp2(s_ref[buf, hd] - jnp.tile(mb_ref[buf, hd], (1, T // LANES)))
            pv = _dot(p.astype(BF16), v_heads[hd])
            if diag:
                accl_all[i, hd] = pv
            else:
                accl_all[i, hd] = al_ref[buf, hd] * accl_all[i, hd] + pv

    def advance(i, jk):
        wrap = jk + 1 >= i
        return jnp.where(wrap, i + 1, i), jnp.where(wrap, 0, jk + 1)

    scores(0, 0, 0, True)

    def diag_body(jj, carry):
        t0 = U * jj
        for u in range(U):
            scores(t0 + u + 1, t0 + u + 1, (u + 1) % 2, True)
            accumulate(t0 + u, t0 + u, u % 2, True)
        return carry

    lax.fori_loop(0, nq // U - 1, diag_body, 0)
    for u in range(U):
        t = nq - U + u
        if u < U - 1:
            scores(t + 1, t + 1, (u + 1) % 2, True)
        else:
            scores(1, 0, 0, False)
        accumulate(t, t, u % 2, True)

    def main_body(_, cur):
        for u in range(U):
            nxt = advance(*cur)
            scores(*nxt, (u + 1) % 2, False)
            accumulate(*cur, u % 2, False)
            cur = nxt
        return cur

    cur = lax.fori_loop(0, n_main // U - 1, main_body, (jnp.int32(1), jnp.int32(0)))
    for u in range(U):
        nxt = advance(*cur)
        if u < U - 1:
            scores(*nxt, (u + 1) % 2, False)
        accumulate(*cur, u % 2, False)
        cur = nxt

    def finish(i, carry):
        outs = []
        for hd in range(2):
            a = accl_all[i, hd]
            outs.append(a * pltpu.roll(1.0 / a, F_DH, axis=1))
        out = jnp.where(first_head, outs[0], outs[1])
        rows = pl.ds(pl.multiple_of(i * T, T), T)
        y_ref[0, rows, :] = (out * _silu(z_ref[0, rows, :].astype(F32))).astype(BF16)
        return carry

    lax.fori_loop(0, nq, finish, 0)


def _fox(u, ft):
    b, s, _ = u.shape
    T = FOX_TQ
    nq = s // T
    blocks_per_group = GROUP_W // LANES
    qb, kb, vb, zb = (slot * blocks_per_group for slot in (U_FQ, U_FK, U_FV, U_FZ))
    seq = lambda first: pl.BlockSpec((1, s, LANES), lambda bi, p: (bi, 0, first + p))
    return pl.pallas_call(
        _fox_kernel,
        grid=(b, F_HEADS // 2),
        in_specs=[
            seq(qb), seq(kb), seq(vb), seq(zb),
            pl.BlockSpec((1, 1, 2, s), lambda bi, p: (bi, p, 0, 0)),
        ],
        out_specs=seq(0),
        out_shape=jax.ShapeDtypeStruct((b, s, F_W), BF16),
        scratch_shapes=[
            pltpu.VMEM((2, 2, T, T), F32),
            pltpu.VMEM((2, 2, T, LANES), F32),
            pltpu.VMEM((2, 2, T, LANES), F32),
            pltpu.VMEM((nq, 2, T, LANES), F32),
            pltpu.VMEM((nq, 2, T, LANES), F32),
        ],
        compiler_params=pltpu.CompilerParams(
            dimension_semantics=("arbitrary", "arbitrary"), vmem_limit_bytes=VMEM_LIMIT),
        name="fox",
    )(u, u, u, u, ft)


def _out_kernel(x_ref, mod_ref, ym_ref, yf_ref, w_ref, o_ref):
    y = _dot(ym_ref[0], w_ref[0:M_W, :]) + _dot(yf_ref[0], w_ref[M_W:, :])
    o_ref[0] = x_ref[0] + mod_ref[0, 2:3, :] * y


def _out(x, mod3, ym, yf, w_out):
    b, s, d = x.shape
    ts = OUT_ROWS
    return pl.pallas_call(
        _out_kernel,
        grid=(b, s // ts),
        in_specs=[
            pl.BlockSpec((1, ts, d), lambda bi, j: (bi, j, 0)),
            pl.BlockSpec((1, 3, d), lambda bi, j: (bi, 0, 0)),
            pl.BlockSpec((1, ts, M_W), lambda bi, j: (bi, j, 0)),
            pl.BlockSpec((1, ts, F_W), lambda bi, j: (bi, j, 0)),
            pl.BlockSpec((M_W + F_W, d), lambda bi, j: (0, 0)),
        ],
        out_specs=pl.BlockSpec((1, ts, d), lambda bi, j: (bi, j, 0)),
        out_shape=jax.ShapeDtypeStruct((b, s, d), x.dtype),
        compiler_params=pltpu.CompilerParams(
            dimension_semantics=("arbitrary", "arbitrary"), vmem_limit_bytes=VMEM_LIMIT),
        name="out_proj",
    )(x, mod3, ym, yf, w_out)


def _layer(x, c_pad, norm_g, w_ada, b_ada, w_in, conv_w, conv_b, b_igate, b_fgate_m,
           mlstm_norm_g, b_fgate_f, fox_qnorm_g, fox_knorm_g, w_out):
    b, s, d = x.shape
    mod = _adaln(c_pad, w_ada, b_ada[None, :])
    mod3 = mod[:b].reshape(b, 3, d)

    n_m = 5 * M_W
    n_f = 4 * F_W
    w_big = jnp.concatenate(
        [w_in[:, :n_m], w_in[:, n_m + 2 * M_HEADS:n_m + 2 * M_HEADS + n_f]], axis=1).astype(BF16)
    w_gate = jnp.concatenate(
        [w_in[:, n_m:n_m + 2 * M_HEADS], w_in[:, n_m + 2 * M_HEADS + n_f:]], axis=1)
    n_gate = w_gate.shape[1]
    w_gate = jnp.pad(w_gate, ((0, 0), (0, LANES - n_gate))).astype(BF16)
    gate_b = jnp.pad(jnp.concatenate([b_igate, b_fgate_m, b_fgate_f]), (0, LANES - n_gate))[None, :]
    qg = jnp.tile(fox_qnorm_g, F_HEADS)[None, :] * (LOG2E / math.sqrt(F_DH))
    kg = jnp.tile(fox_knorm_g, F_HEADS)[None, :]
    head_id = jnp.arange(F_W // 2) // F_DH
    hsum = (head_id[:, None] == head_id[None, :]).astype(BF16)

    u, kt, gc, gtm, ft = _proj(x, mod3, norm_g[None, :], w_big, w_gate, conv_w, conv_b[None, :],
                           gate_b, qg, kg, hsum)
    ym = _mlstm(u, kt, gc, gtm, mlstm_norm_g[None, :])
    yf = _fox(u, ft)
    return _out(x, mod3, ym, yf, w_out.astype(BF16))


def kernel(x, c, norm_g, w_ada, b_ada, w_in, conv_w, conv_b, b_igate, b_fgate_m, mlstm_norm_g,
           b_fgate_f, fox_qnorm_g, fox_knorm_g, w_out):
    depth = norm_g.shape[0]
    b = x.shape[0]
    c_pad = jnp.pad(c, ((0, (-b) % SUBLANES), (0, 0)))
    for l in range(depth):
        x = _layer(x, c_pad, norm_g[l], w_ada[l], b_ada[l], w_in[l], conv_w[l], conv_b[l],
                   b_igate[l], b_fgate_m[l], mlstm_norm_g[l], b_fgate_f[l], fox_qnorm_g[l],
                   fox_knorm_g[l], w_out[l])
    return x
```

```python
import functools
import math

import jax
import jax.numpy as jnp
from jax import lax
from jax.experimental import pallas as pl
from jax.experimental.pallas import tpu as pltpu

F32 = jnp.float32
BF16 = jnp.bfloat16

EPS = 1e-6
M_HEADS = 4
M_DH = 128
F_HEADS = 8
F_DH = 64
CONV_K = 4
M_W = M_HEADS * M_DH
F_W = F_HEADS * F_DH
GROUP_W = 512
W_MQ, W_MK, W_MV, W_MO, W_MZ, W_FQ, W_FK, W_FV, W_FZ = range(9)
U_MQ, U_MV, U_MO, U_MZ, U_FQ, U_FK, U_FV, U_FZ = range(8)
N_SLOTS = 8
LANES = 128
SUBLANES = 8
LOG2E = 1.4426950408889634
NEG_BIG = -1e30

PROJ_ROWS = 512
M_CHUNK = 256
M_STEP_ROWS = 2 * M_CHUNK
FOX_TQ = 512
FOX_UNROLL = 8
VMEM_LIMIT = 56 * 1024 * 1024


def _dot(a, b):
    return jnp.dot(a, b, preferred_element_type=F32)


def _dot_nt(a, b):
    return lax.dot_general(a, b, (((1,), (1,)), ((), ())), preferred_element_type=F32)


def _silu(x):
    return x * jax.nn.sigmoid(x)


def _log_sigmoid(x):
    return jnp.minimum(x, 0.0) - jnp.log1p(jnp.exp(-jnp.abs(x)))


def _adaln_kernel(c_ref, w_ref, b_ref, o_ref):
    c = c_ref[...]
    w = w_ref[...]
    c_hi = c.astype(BF16)
    c_lo = (c - c_hi.astype(F32)).astype(BF16)
    w_hi = w.astype(BF16)
    w_lo = (w - w_hi.astype(F32)).astype(BF16)
    acc = _dot(c_hi, w_hi) + _dot(c_hi, w_lo) + _dot(c_lo, w_hi)
    o_ref[...] = acc + b_ref[...]


def _adaln(c_pad, w_ada, b_ada):
    rows, d = c_pad.shape
    n = w_ada.shape[1]
    tn = d
    return pl.pallas_call(
        _adaln_kernel,
        grid=(n // tn,),
        in_specs=[
            pl.BlockSpec((rows, d), lambda j: (0, 0)),
            pl.BlockSpec((d, tn), lambda j: (0, j)),
            pl.BlockSpec((1, tn), lambda j: (0, j)),
        ],
        out_specs=pl.BlockSpec((rows, tn), lambda j: (0, j)),
        out_shape=jax.ShapeDtypeStruct((rows, n), F32),
        compiler_params=pltpu.CompilerParams(
            dimension_semantics=("arbitrary",), vmem_limit_bytes=VMEM_LIMIT),
        name="adaln",
    )(c_pad, w_ada, b_ada)


def _seg_cumsum(x, seg_row, n):
    shift = 1
    while shift < n:
        rolled = pltpu.roll(x, shift, axis=0)
        x = x + jnp.where(seg_row >= shift, rolled, 0.0)
        shift *= 2
    return x


def _proj_kernel(x_ref, mod_ref, ng_ref, w_ref, wg_ref, cw_ref, cb_ref, gb_ref, qg_ref, kg_ref,
                 hsum_ref, u_ref, kt_ref, gc_ref, gtm_ref, ft_ref, ext_ref, cum_ref):
    j = pl.program_id(1)
    ts = x_ref.shape[1]

    @pl.when(j == 0)
    def _():
        ext_ref[0:SUBLANES, :] = jnp.zeros((SUBLANES, ext_ref.shape[1]), F32)
        cum_ref[...] = jnp.zeros_like(cum_ref)

    x = x_ref[0]
    ms = jnp.mean(x * x, axis=-1, keepdims=True)
    shift = mod_ref[0, 0:1, :]
    gain = ng_ref[...] * (1.0 + mod_ref[0, 1:2, :])
    h = (x * lax.rsqrt(ms + EPS) * gain + shift).astype(BF16)

    def group(g):
        return _dot(h, w_ref[:, g * GROUP_W:(g + 1) * GROUP_W])

    def store(slot, val):
        u_ref[0, :, slot * GROUP_W:(slot + 1) * GROUP_W] = val.astype(BF16)

    ug = _dot(h, wg_ref[...]) + gb_ref[...]
    lane = lax.broadcasted_iota(jnp.int32, ug.shape, 1)
    row = lax.broadcasted_iota(jnp.int32, ug.shape, 0)
    lf = _log_sigmoid(ug) * LOG2E
    seg_row = jnp.where(lane < 2 * M_HEADS, row & (M_CHUNK - 1), row)
    cs = _seg_cumsum(lf, seg_row, ts)
    cs = cs + jnp.where(lane >= 2 * M_HEADS, cum_ref[0:1, :], 0.0)
    cum_ref[0:1, :] = cs[ts - 1:ts, :]
    gc = jnp.where(lane < M_HEADS, ug * LOG2E, cs)
    gc_ref[0] = gc
    gt = gc.T
    gtm_ref[0] = gt[0:2 * M_HEADS, :]
    for p in range(F_HEADS // 2):
        lo = 2 * M_HEADS + 2 * p
        ft_ref[0, p] = gt[lo:lo + 2, :]

    for g, slot, gain_ref in ((W_FQ, U_FQ, qg_ref), (W_FK, U_FK, kg_ref)):
        u = group(g)
        sq = (u * u).astype(BF16)
        half = hsum_ref.shape[0]
        ssq = jnp.concatenate([_dot(sq[:, 0:half], hsum_ref[...]), _dot(sq[:, half:], hsum_ref[...])], axis=1)
        store(slot, u * lax.rsqrt(ssq * (1.0 / F_DH) + EPS) * gain_ref[...])

    ext_ref[SUBLANES:, 0:GROUP_W] = group(W_MQ)
    ext_ref[SUBLANES:, GROUP_W:] = group(W_MK)
    conv = cb_ref[...] + cw_ref[0:1, :] * ext_ref[SUBLANES - 3:SUBLANES - 3 + ts, :]
    for t in range(1, CONV_K):
        lo = SUBLANES - 3 + t
        conv = conv + cw_ref[t:t + 1, :] * ext_ref[lo:lo + ts, :]
    ext_ref[0:SUBLANES, :] = ext_ref[ts:ts + SUBLANES, :]
    qk = _silu(conv)
    store(U_MQ, qk[:, 0:GROUP_W])
    kt_ref[0] = (qk[:, GROUP_W:] * (1.0 / math.sqrt(M_DH))).T.astype(BF16)

    for g, slot in ((W_MV, U_MV), (W_MO, U_MO), (W_MZ, U_MZ), (W_FV, U_FV), (W_FZ, U_FZ)):
        store(slot, group(g))


def _proj(x, mod3, norm_g, w_big, w_gate, conv_w, conv_b, gate_b, qg, kg, hsum):
    b, s, d = x.shape
    ts = PROJ_ROWS
    n_big = w_big.shape[1]
    const = lambda *shape: pl.BlockSpec(shape, lambda bi, j: (0,) * len(shape))
    return pl.pallas_call(
        _proj_kernel,
        grid=(b, s // ts),
        in_specs=[
            pl.BlockSpec((1, ts, d), lambda bi, j: (bi, j, 0)),
            pl.BlockSpec((1, 3, d), lambda bi, j: (bi, 0, 0)),
            const(1, d),
            const(d, n_big),
            const(d, LANES),
            const(CONV_K, 2 * M_W),
            const(1, 2 * M_W),
            const(1, LANES),
            const(1, F_W),
            const(1, F_W),
            const(F_W // 2, F_W // 2),
        ],
        out_specs=[
            pl.BlockSpec((1, ts, N_SLOTS * GROUP_W), lambda bi, j: (bi, j, 0)),
            pl.BlockSpec((1, M_W, ts), lambda bi, j: (bi, 0, j)),
            pl.BlockSpec((1, ts, LANES), lambda bi, j: (bi, j, 0)),
            pl.BlockSpec((1, 2 * M_HEADS, ts), lambda bi, j: (bi, 0, j)),
            pl.BlockSpec((1, F_HEADS // 2, 2, ts), lambda bi, j: (bi, 0, 0, j)),
        ],
        out_shape=[
            jax.ShapeDtypeStruct((b, s, N_SLOTS * GROUP_W), BF16),
            jax.ShapeDtypeStruct((b, M_W, s), BF16),
            jax.ShapeDtypeStruct((b, s, LANES), F32),
            jax.ShapeDtypeStruct((b, 2 * M_HEADS, s), F32),
            jax.ShapeDtypeStruct((b, F_HEADS // 2, 2, s), F32),
        ],
        scratch_shapes=[
            pltpu.VMEM((ts + SUBLANES, 2 * M_W), F32),
            pltpu.VMEM((SUBLANES, LANES), F32),
        ],
        compiler_params=pltpu.CompilerParams(
            dimension_semantics=("arbitrary", "arbitrary"), vmem_limit_bytes=VMEM_LIMIT),
        name="proj",
    )(x, mod3, norm_g, w_big, w_gate, conv_w, conv_b, gate_b, qg, kg, hsum)


def _mlstm_out_kernel(q_ref, kt_ref, v_ref, o_ref, z_ref, gc_ref, gt_ref, lng_ref, x_ref, mod_ref, yf_ref, w_ref,
                      out_ref, cn_ref, m_ref, ym_ref):
    c = pl.program_id(1)
    L = M_CHUNK

    @pl.when(c == 0)
    def _():
        cn_ref[...] = jnp.zeros_like(cn_ref)
        m_ref[...] = jnp.zeros_like(m_ref)

    rows = lax.broadcasted_iota(jnp.int32, (L, L), 0)
    cols = lax.broadcasted_iota(jnp.int32, (L, L), 1)
    causal = cols <= rows
    ones_blk = jnp.ones((L, M_DH), BF16)

    for ck, hd in [(ck, hd) for ck in range(q_ref.shape[1] // L) for hd in range(M_HEADS)]:
        t = slice(ck * L, (ck + 1) * L)
        sl = slice(hd * M_DH, (hd + 1) * M_DH)
        q = q_ref[0, t, sl]
        kt = kt_ref[0, sl, t]
        v1 = jnp.concatenate([v_ref[0, t, sl], ones_blk], axis=1)
        b_c = jnp.broadcast_to(gc_ref[0, t, M_HEADS + hd:M_HEADS + hd + 1], (L, LANES))
        r_row = gt_ref[0, hd:hd + 1, t] - gt_ref[0, M_HEADS + hd:M_HEADS + hd + 1, t]
        m_prev = m_ref[hd:hd + 1, :]
        cn = cn_ref[hd]

        rmat = jnp.where(causal, jnp.broadcast_to(r_row, (L, L)), NEG_BIG)
        g = jnp.maximum(m_prev, jnp.max(rmat, axis=-1, keepdims=True))
        sm = _dot(q, kt) * jnp.exp2(rmat - jnp.tile(g, (1, L // LANES)))
        w_inter = jnp.exp2(m_prev - g)
        numden = jnp.tile(w_inter, (1, 2)) * _dot(q, cn.astype(BF16)) + _dot(sm.astype(BF16), v1)
        e_neg_m = jnp.exp2(-(b_c + g))
        den = jnp.maximum(jnp.abs(numden[:, M_DH:]), e_neg_m)

        g_last = g[L - 1:L, :]
        ws_row = jnp.exp2(r_row - jnp.tile(g_last, (1, L // LANES)))
        ktw = kt * ws_row.astype(BF16)
        decay = jnp.exp2(m_prev - g_last)
        cn_ref[hd] = jnp.tile(decay, (1, 2)) * cn + _dot(ktw, v1)
        m_ref[hd:hd + 1, :] = b_c[L - 1:L, :] + g_last

        hm = numden[:, 0:M_DH] / (den * (1.0 + jnp.exp2(o_ref[0, t, sl].astype(F32) * (-LOG2E))))
        mu = jnp.mean(hm, axis=-1, keepdims=True)
        dv = hm - mu
        var = jnp.mean(dv * dv, axis=-1, keepdims=True)
        y = dv * lax.rsqrt(var + EPS) * lng_ref[:, sl]
        ym_ref[t, sl] = (y * _silu(z_ref[0, t, sl].astype(F32))).astype(BF16)

    proj = _dot(ym_ref[...], w_ref[0:M_W, :]) + _dot(yf_ref[0], w_ref[M_W:, :])
    out_ref[0] = x_ref[0] + mod_ref[0, 2:3, :] * proj


def _mlstm_out(u, kt, gc, gtm, ln_g, x, mod3, yf, w_out):
    b, s, d = x.shape
    L = M_STEP_ROWS
    col = lambda g: pl.BlockSpec((1, L, GROUP_W), lambda bi, c, g=g: (bi, c, g))
    rows = lambda width: pl.BlockSpec((1, L, width), lambda bi, c: (bi, c, 0))
    return pl.pallas_call(
        _mlstm_out_kernel,
        grid=(b, s // L),
        in_specs=[
            col(U_MQ),
            pl.BlockSpec((1, M_W, L), lambda bi, c: (bi, 0, c)),
            col(U_MV), col(U_MO), col(U_MZ),
            rows(LANES),
            pl.BlockSpec((1, 2 * M_HEADS, L), lambda bi, c: (bi, 0, c)),
            pl.BlockSpec((1, M_W), lambda bi, c: (0, 0)),
            rows(d),
            pl.BlockSpec((1, 3, d), lambda bi, c: (bi, 0, 0)),
            rows(F_W),
            pl.BlockSpec((M_W + F_W, d), lambda bi, c: (0, 0)),
        ],
        out_specs=rows(d),
        out_shape=jax.ShapeDtypeStruct((b, s, d), x.dtype),
        scratch_shapes=[
            pltpu.VMEM((M_HEADS, M_DH, 2 * M_DH), F32),
            pltpu.VMEM((SUBLANES, LANES), F32),
            pltpu.VMEM((L, M_W), BF16),
        ],
        compiler_params=pltpu.CompilerParams(
            dimension_semantics=("arbitrary", "arbitrary"), vmem_limit_bytes=VMEM_LIMIT),
        name="mlstm_out",
    )(u, kt, u, u, u, gc, gtm, ln_g, x, mod3, yf, w_out)


def _fox_kernel(q_ref, k_ref, v_ref, z_ref, ft_ref, y_ref, s_ref, mb_ref, al_ref, m_all, accl_all):
    T = FOX_TQ
    nq = q_ref.shape[1] // T
    n_main = nq * (nq - 1) // 2
    U = FOX_UNROLL
    assert U % 2 == 0 and nq % U == 0 and n_main % U == 0 and n_main >= 2 * U
    first_head = lax.broadcasted_iota(jnp.int32, (T, LANES), 1) < F_DH

    def block(ref, blk):
        return ref[0, pl.ds(pl.multiple_of(blk * T, T), T), :]

    def f_rows(blk):
        return ft_ref[0, 0, :, pl.ds(pl.multiple_of(blk * T, T), T)]

    def scores(i, jk, buf, diag):
        q = block(q_ref, i)
        zero = jnp.zeros_like(q)
        q_heads = (jnp.where(first_head, q, zero), jnp.where(first_head, zero, q))
        k = block(k_ref, jk)
        bias = f_rows(i)[:, 0:1] - f_rows(jk)
        for hd in range(2):
            s = _dot_nt(q_heads[hd], k) + bias[hd:hd + 1, :]
            if diag:
                rows = lax.broadcasted_iota(jnp.int32, (T, T), 0)
                cols = lax.broadcasted_iota(jnp.int32, (T, T), 1)
                s = jnp.where(cols <= rows, s, NEG_BIG)
            row_max = jnp.max(s, axis=-1, keepdims=True)
            if diag:
                m_new = jnp.broadcast_to(row_max, (T, LANES))
            else:
                m_old = m_all[i, hd]
                m_new = jnp.maximum(m_old, row_max)
                al_ref[buf, hd] = jnp.exp2(m_old - m_new)
            m_all[i, hd] = m_new
            mb_ref[buf, hd] = m_new
            s_ref[buf, hd] = s

    def accumulate(i, jk, buf, diag):
        v = block(v_ref, jk)
        one = jnp.ones_like(v)
        v_heads = (jnp.where(first_head, v, one), jnp.where(first_head, one, v))
        for hd in range(2):
            p = jnp.exp2(s_ref[buf, hd] - jnp.tile(mb_ref[buf, hd], (1, T // LANES)))
            pv = _dot(p.astype(BF16), v_heads[hd])
            if diag:
                accl_all[i, hd] = pv
            else:
                accl_all[i, hd] = al_ref[buf, hd] * accl_all[i, hd] + pv

    def advance(i, jk):
        wrap = jk + 1 >= i
        return jnp.where(wrap, i + 1, i), jnp.where(wrap, 0, jk + 1)

    scores(0, 0, 0, True)

    def diag_body(jj, carry):
        t0 = U * jj
        for u in range(U):
            scores(t0 + u + 1, t0 + u + 1, (u + 1) % 2, True)
            accumulate(t0 + u, t0 + u, u % 2, True)
        return carry

    lax.fori_loop(0, nq // U - 1, diag_body, 0)
    for u in range(U):
        t = nq - U + u
        if u < U - 1:
            scores(t + 1, t + 1, (u + 1) % 2, True)
        else:
            scores(1, 0, 0, False)
        accumulate(t, t, u % 2, True)

    def main_body(_, cur):
        for u in range(U):
            nxt = advance(*cur)
            scores(*nxt, (u + 1) % 2, False)
            accumulate(*cur, u % 2, False)
            cur = nxt
        return cur

    cur = lax.fori_loop(0, n_main // U - 1, main_body, (jnp.int32(1), jnp.int32(0)))
    for u in range(U):
        nxt = advance(*cur)
        if u < U - 1:
            scores(*nxt, (u + 1) % 2, False)
        accumulate(*cur, u % 2, False)
        cur = nxt

    def finish(i, carry):
        outs = []
        for hd in range(2):
            a = accl_all[i, hd]
            outs.append(a * pltpu.roll(1.0 / a, F_DH, axis=1))
        out = jnp.where(first_head, outs[0], outs[1])
        rows = pl.ds(pl.multiple_of(i * T, T), T)
        y_ref[0, rows, :] = (out * _silu(z_ref[0, rows, :].astype(F32))).astype(BF16)
        return carry

    lax.fori_loop(0, nq, finish, 0)


def _fox(u, ft):
    b, s, _ = u.shape
    T = FOX_TQ
    nq = s // T
    blocks_per_group = GROUP_W // LANES
    qb, kb, vb, zb = (slot * blocks_per_group for slot in (U_FQ, U_FK, U_FV, U_FZ))
    seq = lambda first: pl.BlockSpec((1, s, LANES), lambda bi, p: (bi, 0, first + p))
    return pl.pallas_call(
        _fox_kernel,
        grid=(b, F_HEADS // 2),
        in_specs=[
            seq(qb), seq(kb), seq(vb), seq(zb),
            pl.BlockSpec((1, 1, 2, s), lambda bi, p: (bi, p, 0, 0)),
        ],
        out_specs=seq(0),
        out_shape=jax.ShapeDtypeStruct((b, s, F_W), BF16),
        scratch_shapes=[
            pltpu.VMEM((2, 2, T, T), F32),
            pltpu.VMEM((2, 2, T, LANES), F32),
            pltpu.VMEM((2, 2, T, LANES), F32),
            pltpu.VMEM((nq, 2, T, LANES), F32),
            pltpu.VMEM((nq, 2, T, LANES), F32),
        ],
        compiler_params=pltpu.CompilerParams(
            dimension_semantics=("arbitrary", "arbitrary"), vmem_limit_bytes=VMEM_LIMIT),
        name="fox",
    )(u, u, u, u, ft)


def _layer(x, c_pad, norm_g, w_ada, b_ada, w_in, conv_w, conv_b, b_igate, b_fgate_m,
           mlstm_norm_g, b_fgate_f, fox_qnorm_g, fox_knorm_g, w_out):
    b, s, d = x.shape
    mod = _adaln(c_pad, w_ada, b_ada[None, :])
    mod3 = mod[:b].reshape(b, 3, d)

    n_m = 5 * M_W
    n_f = 4 * F_W
    w_big = jnp.concatenate(
        [w_in[:, :n_m], w_in[:, n_m + 2 * M_HEADS:n_m + 2 * M_HEADS + n_f]], axis=1).astype(BF16)
    w_gate = jnp.concatenate(
        [w_in[:, n_m:n_m + 2 * M_HEADS], w_in[:, n_m + 2 * M_HEADS + n_f:]], axis=1)
    n_gate = w_gate.shape[1]
    w_gate = jnp.pad(w_gate, ((0, 0), (0, LANES - n_gate))).astype(BF16)
    gate_b = jnp.pad(jnp.concatenate([b_igate, b_fgate_m, b_fgate_f]), (0, LANES - n_gate))[None, :]
    qg = jnp.tile(fox_qnorm_g, F_HEADS)[None, :] * (LOG2E / math.sqrt(F_DH))
    kg = jnp.tile(fox_knorm_g, F_HEADS)[None, :]
    head_id = jnp.arange(F_W // 2) // F_DH
    hsum = (head_id[:, None] == head_id[None, :]).astype(BF16)

    u, kt, gc, gtm, ft = _proj(x, mod3, norm_g[None, :], w_big, w_gate, conv_w, conv_b[None, :],
                           gate_b, qg, kg, hsum)
    yf = _fox(u, ft)
    return _mlstm_out(u, kt, gc, gtm, mlstm_norm_g[None, :], x, mod3, yf, w_out.astype(BF16))


def kernel(x, c, norm_g, w_ada, b_ada, w_in, conv_w, conv_b, b_igate, b_fgate_m, mlstm_norm_g,
           b_fgate_f, fox_qnorm_g, fox_knorm_g, w_out):
    depth = norm_g.shape[0]
    b = x.shape[0]
    c_pad = jnp.pad(c, ((0, (-b) % SUBLANES), (0, 0)))
    for l in range(depth):
        x = _layer(x, c_pad, norm_g[l], w_ada[l], b_ada[l], w_in[l], conv_w[l], conv_b[l],
                   b_igate[l], b_fgate_m[l], mlstm_norm_g[l], b_fgate_f[l], fox_qnorm_g[l],
                   fox_knorm_g[l], w_out[l])
    return x
```

```python
import math

import jax
import jax.numpy as jnp
import numpy as np
from jax import lax
from jax.experimental import pallas as pl
from jax.experimental.pallas import tpu as pltpu

F32 = jnp.float32
BF16 = jnp.bfloat16

EPS = 1e-6
M_HEADS = 4
M_DH = 128
F_HEADS = 8
F_DH = 64
CONV_K = 4
M_W = M_HEADS * M_DH
F_W = F_HEADS * F_DH
GROUP_W = 512
W_MQ, W_MK, W_MV, W_MO, W_MZ, W_FQ, W_FK, W_FV, W_FZ = range(9)
U_MQ, U_MV, U_MO, U_MZ, U_FQ, U_FK, U_FV, U_FZ = range(8)
N_SLOTS = 8
LANES = 128
SUBLANES = 8
LOG2E = 1.4426950408889634
NEG_BIG = -1e30

PROJ_ROWS = 512
M_CHUNK = 256
M_STEP_ROWS = 2 * M_CHUNK
FOX_TQ = 512
FOX_UNROLL = 8
AUG_LANES = LANES // F_HEADS
VMEM_LIMIT = 56 * 1024 * 1024


def _dot(a, b):
    return jnp.dot(a, b, preferred_element_type=F32)


def _dot_nt(a, b):
    return lax.dot_general(a, b, (((1,), (1,)), ((), ())), preferred_element_type=F32)


def _silu(x):
    return x * jax.nn.sigmoid(x)


def _log_sigmoid(x):
    return jnp.minimum(x, 0.0) - jnp.log1p(jnp.exp(-jnp.abs(x)))


def _adaln_kernel(c_ref, w_ref, b_ref, o_ref):
    c = c_ref[...]
    w = w_ref[...]
    c_hi = c.astype(BF16)
    c_lo = (c - c_hi.astype(F32)).astype(BF16)
    w_hi = w.astype(BF16)
    w_lo = (w - w_hi.astype(F32)).astype(BF16)
    acc = _dot(c_hi, w_hi) + _dot(c_hi, w_lo) + _dot(c_lo, w_hi)
    o_ref[...] = acc + b_ref[...]


def _adaln(c_pad, w_ada, b_ada):
    rows, d = c_pad.shape
    n = w_ada.shape[1]
    tn = d
    return pl.pallas_call(
        _adaln_kernel,
        grid=(n // tn,),
        in_specs=[
            pl.BlockSpec((rows, d), lambda j: (0, 0)),
            pl.BlockSpec((d, tn), lambda j: (0, j)),
            pl.BlockSpec((1, tn), lambda j: (0, j)),
        ],
        out_specs=pl.BlockSpec((rows, tn), lambda j: (0, j)),
        out_shape=jax.ShapeDtypeStruct((rows, n), F32),
        compiler_params=pltpu.CompilerParams(
            dimension_semantics=("arbitrary",), vmem_limit_bytes=VMEM_LIMIT),
        name="adaln",
    )(c_pad, w_ada, b_ada)


def _seg_cumsum(x, seg_row, n):
    shift = 1
    while shift < n:
        rolled = pltpu.roll(x, shift, axis=0)
        x = x + jnp.where(seg_row >= shift, rolled, 0.0)
        shift *= 2
    return x


def _proj_kernel(x_ref, mod_ref, ng_ref, wm_ref, wf_ref, wg_ref, cw_ref, cb_ref, gb_ref, qg_ref, kg_ref,
                 hsum_ref, sel_ref, aug1_ref, u_ref, kt_ref, gc_ref, gtm_ref, qaug_ref, kaug_ref, ext_ref, cum_ref):
    j = pl.program_id(1)
    ts = x_ref.shape[1]

    @pl.when(j == 0)
    def _():
        ext_ref[0:SUBLANES, :] = jnp.zeros((SUBLANES, ext_ref.shape[1]), F32)
        cum_ref[...] = jnp.zeros_like(cum_ref)

    x = x_ref[0]
    ms = jnp.mean(x * x, axis=-1, keepdims=True)
    shift = mod_ref[0, 0:1, :]
    gain = ng_ref[...] * (1.0 + mod_ref[0, 1:2, :])
    h = (x * lax.rsqrt(ms + EPS) * gain + shift).astype(BF16)

    def group(g):
        w_ref, first = (wm_ref, W_MQ) if g < W_FQ else (wf_ref, W_FQ)
        return _dot(h, w_ref[:, (g - first) * GROUP_W:(g - first + 1) * GROUP_W])

    def store(slot, val):
        u_ref[0, :, slot * GROUP_W:(slot + 1) * GROUP_W] = val.astype(BF16)

    ug = _dot(h, wg_ref[...]) + gb_ref[...]
    lane = lax.broadcasted_iota(jnp.int32, ug.shape, 1)
    row = lax.broadcasted_iota(jnp.int32, ug.shape, 0)
    lf = _log_sigmoid(ug) * LOG2E
    seg_row = jnp.where(lane < 2 * M_HEADS, row & (M_CHUNK - 1), row)
    cs = _seg_cumsum(lf, seg_row, ts)
    cs = cs + jnp.where(lane >= 2 * M_HEADS, cum_ref[0:1, :], 0.0)
    cum_ref[0:1, :] = cs[ts - 1:ts, :]
    gc = jnp.where(lane < M_HEADS, ug * LOG2E, cs)
    gc_ref[0] = gc
    gtm_ref[0] = gc.T[0:2 * M_HEADS, :]

    for g, slot, gain_ref in ((W_FQ, U_FQ, qg_ref), (W_FK, U_FK, kg_ref)):
        u = group(g)
        sq = (u * u).astype(BF16)
        half = hsum_ref.shape[0]
        ssq = jnp.concatenate([_dot(sq[:, 0:half], hsum_ref[...]), _dot(sq[:, half:], hsum_ref[...])], axis=1)
        store(slot, u * lax.rsqrt(ssq * (1.0 / F_DH) + EPS) * gain_ref[...])

    ext_ref[SUBLANES:, 0:GROUP_W] = group(W_MQ)
    ext_ref[SUBLANES:, GROUP_W:] = group(W_MK)
    conv = cb_ref[...] + cw_ref[0:1, :] * ext_ref[SUBLANES - 3:SUBLANES - 3 + ts, :]
    for t in range(1, CONV_K):
        lo = SUBLANES - 3 + t
        conv = conv + cw_ref[t:t + 1, :] * ext_ref[lo:lo + ts, :]
    ext_ref[0:SUBLANES, :] = ext_ref[ts:ts + SUBLANES, :]
    qk = _silu(conv)
    store(U_MQ, qk[:, 0:GROUP_W])
    kt_ref[0] = (qk[:, GROUP_W:] * (1.0 / math.sqrt(M_DH))).T.astype(BF16)

    for g, slot in ((W_MV, U_MV), (W_MO, U_MO), (W_MZ, U_MZ), (W_FV, U_FV), (W_FZ, U_FZ)):
        store(slot, group(g))

    def split3(a):
        hi = a.astype(BF16)
        r1 = a - hi.astype(F32)
        mid = r1.astype(BF16)
        lo = (r1 - mid.astype(F32)).astype(BF16)
        return jnp.concatenate([hi, mid, lo], axis=1)

    f_blk = jnp.broadcast_to(cs[0:1, :], (SUBLANES, LANES))
    blk_parts = split3(f_blk)
    k_const = _dot(blk_parts, sel_ref[1])[0:1, :] + aug1_ref[0:1, :]
    q_const = _dot(blk_parts, sel_ref[2])[0:1, :] + aug1_ref[1:2, :]
    kaug_ref[0] = (_dot(split3(cs[0:1, :] - cs), sel_ref[0]) + k_const).astype(BF16)
    qaug_ref[0] = jnp.broadcast_to(q_const, (ts, LANES)).astype(BF16)


def _proj(x, mod3, norm_g, w_m, w_f, w_gate, conv_w, conv_b, gate_b, qg, kg, hsum, sel, aug1):
    b, s, d = x.shape
    ts = PROJ_ROWS
    const = lambda *shape: pl.BlockSpec(shape, lambda bi, j: (0,) * len(shape))
    return pl.pallas_call(
        _proj_kernel,
        grid=(b, s // ts),
        in_specs=[
            pl.BlockSpec((1, ts, d), lambda bi, j: (bi, j, 0)),
            pl.BlockSpec((1, 3, d), lambda bi, j: (bi, 0, 0)),
            const(1, d),
            const(d, w_m.shape[1]),
            const(d, w_f.shape[1]),
            const(d, LANES),
            const(CONV_K, 2 * M_W),
            const(1, 2 * M_W),
            const(1, LANES),
            const(1, F_W),
            const(1, F_W),
            const(F_W // 2, F_W // 2),
            const(3, 3 * LANES, LANES),
            const(2, LANES),
        ],
        out_specs=[
            pl.BlockSpec((1, ts, N_SLOTS * GROUP_W), lambda bi, j: (bi, j, 0)),
            pl.BlockSpec((1, M_W, ts), lambda bi, j: (bi, 0, j)),
            pl.BlockSpec((1, ts, LANES), lambda bi, j: (bi, j, 0)),
            pl.BlockSpec((1, 2 * M_HEADS, ts), lambda bi, j: (bi, 0, j)),
            pl.BlockSpec((1, ts, LANES), lambda bi, j: (bi, j, 0)),
            pl.BlockSpec((1, ts, LANES), lambda bi, j: (bi, j, 0)),
        ],
        out_shape=[
            jax.ShapeDtypeStruct((b, s, N_SLOTS * GROUP_W), BF16),
            jax.ShapeDtypeStruct((b, M_W, s), BF16),
            jax.ShapeDtypeStruct((b, s, LANES), F32),
            jax.ShapeDtypeStruct((b, 2 * M_HEADS, s), F32),
            jax.ShapeDtypeStruct((b, s, LANES), BF16),
            jax.ShapeDtypeStruct((b, s, LANES), BF16),
        ],
        scratch_shapes=[
            pltpu.VMEM((ts + SUBLANES, 2 * M_W), F32),
            pltpu.VMEM((SUBLANES, LANES), F32),
        ],
        compiler_params=pltpu.CompilerParams(
            dimension_semantics=("arbitrary", "arbitrary"), vmem_limit_bytes=VMEM_LIMIT),
        name="proj",
    )(x, mod3, norm_g, w_m, w_f, w_gate, conv_w, conv_b, gate_b, qg, kg, hsum, sel, aug1)


def _mlstm_out_kernel(q_ref, kt_ref, v_ref, o_ref, z_ref, gc_ref, gt_ref, lng_ref, x_ref, mod_ref, yf_ref, w_ref,
                      out_ref, cn_ref, m_ref, ym_ref):
    c = pl.program_id(1)
    L = M_CHUNK

    @pl.when(c == 0)
    def _():
        cn_ref[...] = jnp.zeros_like(cn_ref)
        m_ref[...] = jnp.zeros_like(m_ref)

    rows = lax.broadcasted_iota(jnp.int32, (L, L), 0)
    cols = lax.broadcasted_iota(jnp.int32, (L, L), 1)
    causal = cols <= rows
    ones_blk = jnp.ones((L, M_DH), BF16)

    for ck, hd in [(ck, hd) for ck in range(q_ref.shape[1] // L) for hd in range(M_HEADS)]:
        t = slice(ck * L, (ck + 1) * L)
        sl = slice(hd * M_DH, (hd + 1) * M_DH)
        q = q_ref[0, t, sl]
        kt = kt_ref[0, sl, t]
        v1 = jnp.concatenate([v_ref[0, t, sl], ones_blk], axis=1)
        b_c = jnp.broadcast_to(gc_ref[0, t, M_HEADS + hd:M_HEADS + hd + 1], (L, LANES))
        r_row = gt_ref[0, hd:hd + 1, t] - gt_ref[0, M_HEADS + hd:M_HEADS + hd + 1, t]
        m_prev = m_ref[hd:hd + 1, :]
        cn = cn_ref[hd]

        rmat = jnp.where(causal, jnp.broadcast_to(r_row, (L, L)), NEG_BIG)
        g = jnp.maximum(m_prev, jnp.max(rmat, axis=-1, keepdims=True))
        sm = _dot(q, kt) * jnp.exp2(rmat - jnp.tile(g, (1, L // LANES)))
        w_inter = jnp.exp2(m_prev - g)
        numden = jnp.tile(w_inter, (1, 2)) * _dot(q, cn.astype(BF16)) + _dot(sm.astype(BF16), v1)
        e_neg_m = jnp.exp2(-(b_c + g))
        den = jnp.maximum(jnp.abs(numden[:, M_DH:]), e_neg_m)

        g_last = g[L - 1:L, :]
        ws_row = jnp.exp2(r_row - jnp.tile(g_last, (1, L // LANES)))
        ktw = kt * ws_row.astype(BF16)
        decay = jnp.exp2(m_prev - g_last)
        cn_ref[hd] = jnp.tile(decay, (1, 2)) * cn + _dot(ktw, v1)
        m_ref[hd:hd + 1, :] = b_c[L - 1:L, :] + g_last

        hm = numden[:, 0:M_DH] / (den * (1.0 + jnp.exp2(o_ref[0, t, sl].astype(F32) * (-LOG2E))))
        mu = jnp.mean(hm, axis=-1, keepdims=True)
        dv = hm - mu
        var = jnp.mean(dv * dv, axis=-1, keepdims=True)
        y = dv * lax.rsqrt(var + EPS) * lng_ref[:, sl]
        ym_ref[t, sl] = (y * _silu(z_ref[0, t, sl].astype(F32))).astype(BF16)

    proj = _dot(ym_ref[...], w_ref[0:M_W, :]) + _dot(yf_ref[0], w_ref[M_W:, :])
    out_ref[0] = x_ref[0] + mod_ref[0, 2:3, :] * proj


def _mlstm_out(u, kt, gc, gtm, ln_g, x, mod3, yf, w_out):
    b, s, d = x.shape
    L = M_STEP_ROWS
    col = lambda g: pl.BlockSpec((1, L, GROUP_W), lambda bi, c, g=g: (bi, c, g))
    rows = lambda width: pl.BlockSpec((1, L, width), lambda bi, c: (bi, c, 0))
    return pl.pallas_call(
        _mlstm_out_kernel,
        grid=(b, s // L),
        in_specs=[
            col(U_MQ),
            pl.BlockSpec((1, M_W, L), lambda bi, c: (bi, 0, c)),
            col(U_MV), col(U_MO), col(U_MZ),
            rows(LANES),
            pl.BlockSpec((1, 2 * M_HEADS, L), lambda bi, c: (bi, 0, c)),
            pl.BlockSpec((1, M_W), lambda bi, c: (0, 0)),
            rows(d),
            pl.BlockSpec((1, 3, d), lambda bi, c: (bi, 0, 0)),
            rows(F_W),
            pl.BlockSpec((M_W + F_W, d), lambda bi, c: (0, 0)),
        ],
        out_specs=rows(d),
        out_shape=jax.ShapeDtypeStruct((b, s, d), x.dtype),
        scratch_shapes=[
            pltpu.VMEM((M_HEADS, M_DH, 2 * M_DH), F32),
            pltpu.VMEM((SUBLANES, LANES), F32),
            pltpu.VMEM((L, M_W), BF16),
        ],
        compiler_params=pltpu.CompilerParams(
            dimension_semantics=("arbitrary", "arbitrary"), vmem_limit_bytes=VMEM_LIMIT),
        name="mlstm_out",
    )(u, kt, u, u, u, gc, gtm, ln_g, x, mod3, yf, w_out)


def _fox_kernel(q_ref, k_ref, v_ref, z_ref, qaug_ref, kaug_ref, y_ref, s_ref, mb_ref, al_ref, m_all, accl_all):
    T = FOX_TQ
    nq = q_ref.shape[1] // T
    n_main = nq * (nq - 1) // 2
    U = FOX_UNROLL
    assert U % 2 == 0 and nq % U == 0 and n_main % U == 0 and n_main >= 2 * U
    first_head = lax.broadcasted_iota(jnp.int32, (T, LANES), 1) < F_DH

    def block(ref, blk):
        return ref[0, pl.ds(pl.multiple_of(blk * T, T), T), :]

    lane = lax.broadcasted_iota(jnp.int32, (T, LANES), 1)
    aug_lo = pl.program_id(1) * (2 * AUG_LANES)
    aug_mask = [(lane - (aug_lo + hd * AUG_LANES)).astype(jnp.uint32) < AUG_LANES for hd in range(2)]

    def scores(i, jk, buf, diag):
        q = block(q_ref, i)
        zero = jnp.zeros_like(q)
        q_heads = (jnp.where(first_head, q, zero), jnp.where(first_head, zero, q))
        k = jnp.concatenate([block(k_ref, jk), block(kaug_ref, jk)], axis=1)
        qa = block(qaug_ref, i)
        for hd in range(2):
            qx = jnp.concatenate([q_heads[hd], jnp.where(aug_mask[hd], qa, zero)], axis=1)
            s = _dot_nt(qx, k)
            if diag:
                rows = lax.broadcasted_iota(jnp.int32, (T, T), 0)
                cols = lax.broadcasted_iota(jnp.int32, (T, T), 1)
                s = jnp.where(cols <= rows, s, NEG_BIG)
            row_max = jnp.max(s, axis=-1, keepdims=True)
            if diag:
                m_new = jnp.broadcast_to(row_max, (T, LANES))
            else:
                m_old = m_all[i, hd]
                m_new = jnp.maximum(m_old, row_max)
                al_ref[buf, hd] = jnp.exp2(m_old - m_new)
            m_all[i, hd] = m_new
            mb_ref[buf, hd] = m_new
            s_ref[buf, hd] = s

    def accumulate(i, jk, buf, diag):
        v = block(v_ref, jk)
        one = jnp.ones_like(v)
        v_heads = (jnp.where(first_head, v, one), jnp.where(first_head, one, v))
        for hd in range(2):
            p = jnp.exp2(s_ref[buf, hd] - jnp.tile(mb_ref[buf, hd], (1, T // LANES)))
            pv = _dot(p.astype(BF16), v_heads[hd])
            if diag:
                accl_all[i, hd] = pv
            else:
                accl_all[i, hd] = al_ref[buf, hd] * accl_all[i, hd] + pv

    def advance(i, jk):
        wrap = jk + 1 >= i
        return jnp.where(wrap, i + 1, i), jnp.where(wrap, 0, jk + 1)

    scores(0, 0, 0, True)

    def diag_body(jj, carry):
        t0 = U * jj
        for u in range(U):
            scores(t0 + u + 1, t0 + u + 1, (u + 1) % 2, True)
            accumulate(t0 + u, t0 + u, u % 2, True)
        return carry

    lax.fori_loop(0, nq // U - 1, diag_body, 0)
    for u in range(U):
        t = nq - U + u
        if u < U - 1:
            scores(t + 1, t + 1, (u + 1) % 2, True)
        else:
            scores(1, 0, 0, False)
        accumulate(t, t, u % 2, True)

    def main_body(_, cur):
        for u in range(U):
            nxt = advance(*cur)
            scores(*nxt, (u + 1) % 2, False)
            accumulate(*cur, u % 2, False)
            cur = nxt
        return cur

    cur = lax.fori_loop(0, n_main // U - 1, main_body, (jnp.int32(1), jnp.int32(0)))
    for u in range(U):
        nxt = advance(*cur)
        if u < U - 1:
            scores(*nxt, (u + 1) % 2, False)
        accumulate(*cur, u % 2, False)
        cur = nxt

    def finish(i, carry):
        outs = []
        for hd in range(2):
            a = accl_all[i, hd]
            outs.append(a * pltpu.roll(1.0 / a, F_DH, axis=1))
        out = jnp.where(first_head, outs[0], outs[1])
        rows = pl.ds(pl.multiple_of(i * T, T), T)
        y_ref[0, rows, :] = (out * _silu(z_ref[0, rows, :].astype(F32))).astype(BF16)
        return carry

    lax.fori_loop(0, nq, finish, 0)


def _fox(u, qaug, kaug):
    b, s, _ = u.shape
    T = FOX_TQ
    nq = s // T
    blocks_per_group = GROUP_W // LANES
    qb, kb, vb, zb = (slot * blocks_per_group for slot in (U_FQ, U_FK, U_FV, U_FZ))
    seq = lambda first: pl.BlockSpec((1, s, LANES), lambda bi, p: (bi, 0, first + p))
    return pl.pallas_call(
        _fox_kernel,
        grid=(b, F_HEADS // 2),
        in_specs=[
            seq(qb), seq(kb), seq(vb), seq(zb),
            pl.BlockSpec((1, s, LANES), lambda bi, p: (bi, 0, 0)),
            pl.BlockSpec((1, s, LANES), lambda bi, p: (bi, 0, 0)),
        ],
        out_specs=seq(0),
        out_shape=jax.ShapeDtypeStruct((b, s, F_W), BF16),
        scratch_shapes=[
            pltpu.VMEM((2, 2, T, T), F32),
            pltpu.VMEM((2, 2, T, LANES), F32),
            pltpu.VMEM((2, 2, T, LANES), F32),
            pltpu.VMEM((nq, 2, T, LANES), F32),
            pltpu.VMEM((nq, 2, T, LANES), F32),
        ],
        compiler_params=pltpu.CompilerParams(
            dimension_semantics=("arbitrary", "arbitrary"), vmem_limit_bytes=VMEM_LIMIT),
        name="fox",
    )(u, u, u, u, qaug, kaug)


def _bias_column_constants():
    sel = np.zeros((3, 3 * LANES, LANES), np.float32)
    aug1 = np.zeros((2, LANES), np.float32)
    for h in range(F_HEADS):
        src_lane = 2 * M_HEADS + h
        base = h * AUG_LANES
        for c in range(3):
            sel[0, c * LANES + src_lane, base + c] = 1.0
            sel[1, c * LANES + src_lane, base + 6 + c] = -1.0
            sel[2, c * LANES + src_lane, base + 3 + c] = 1.0
            aug1[0, base + 3 + c] = 1.0
            aug1[1, base + c] = 1.0
            aug1[1, base + 6 + c] = 1.0
    return jnp.asarray(sel, BF16), jnp.asarray(aug1, F32)


def _layer(x, c_pad, norm_g, w_ada, b_ada, w_in, conv_w, conv_b, b_igate, b_fgate_m,
           mlstm_norm_g, b_fgate_f, fox_qnorm_g, fox_knorm_g, w_out):
    b, s, d = x.shape
    mod = _adaln(c_pad, w_ada, b_ada[None, :])
    mod3 = mod[:b].reshape(b, 3, d)

    n_m = 5 * M_W
    n_f = 4 * F_W
    w_m = w_in[:, :n_m].astype(BF16)
    w_f = w_in[:, n_m + 2 * M_HEADS:n_m + 2 * M_HEADS + n_f].astype(BF16)
    w_gate = jnp.concatenate(
        [w_in[:, n_m:n_m + 2 * M_HEADS], w_in[:, n_m + 2 * M_HEADS + n_f:]], axis=1)
    n_gate = w_gate.shape[1]
    w_gate = jnp.pad(w_gate, ((0, 0), (0, LANES - n_gate))).astype(BF16)
    gate_b = jnp.pad(jnp.concatenate([b_igate, b_fgate_m, b_fgate_f]), (0, LANES - n_gate))[None, :]
    qg = jnp.tile(fox_qnorm_g, F_HEADS)[None, :] * (LOG2E / math.sqrt(F_DH))
    kg = jnp.tile(fox_knorm_g, F_HEADS)[None, :]
    head_id = jnp.arange(F_W // 2) // F_DH
    hsum = (head_id[:, None] == head_id[None, :]).astype(BF16)

    assert PROJ_ROWS == FOX_TQ
    sel, aug1 = _bias_column_constants()
    u, kt, gc, gtm, qaug, kaug = _proj(x, mod3, norm_g[None, :], w_m, w_f, w_gate, conv_w, conv_b[None, :],
                                       gate_b, qg, kg, hsum, sel, aug1)
    yf = _fox(u, qaug, kaug)
    return _mlstm_out(u, kt, gc, gtm, mlstm_norm_g[None, :], x, mod3, yf, w_out.astype(BF16))


def kernel(x, c, norm_g, w_ada, b_ada, w_in, conv_w, conv_b, b_igate, b_fgate_m, mlstm_norm_g,
           b_fgate_f, fox_qnorm_g, fox_knorm_g, w_out):
    depth = norm_g.shape[0]
    b = x.shape[0]
    c_pad = jnp.pad(c, ((0, (-b) % SUBLANES), (0, 0)))
    for l in range(depth):
        x = _layer(x, c_pad, norm_g[l], w_ada[l], b_ada[l], w_in[l], conv_w[l], conv_b[l],
                   b_igate[l], b_fgate_m[l], mlstm_norm_g[l], b_fgate_f[l], fox_qnorm_g[l],
                   fox_knorm_g[l], w_out[l])
    return x
```

```python
import math

import jax
import jax.numpy as jnp
import numpy as np
from jax import lax
from jax.experimental import pallas as pl
from jax.experimental.pallas import tpu as pltpu

F32 = jnp.float32
BF16 = jnp.bfloat16

EPS = 1e-6
M_HEADS = 4
M_DH = 128
F_HEADS = 8
F_DH = 64
CONV_K = 4
M_W = M_HEADS * M_DH
F_W = F_HEADS * F_DH
GROUP_W = 512
W_MQ, W_MK, W_MV, W_MO, W_MZ, W_FQ, W_FK, W_FV, W_FZ = range(9)
U_MQ, U_MV, U_MO, U_MZ, U_FQ, U_FK, U_FV, U_FZ = range(8)
N_SLOTS = 8
LANES = 128
SUBLANES = 8
LOG2E = 1.4426950408889634
NEG_BIG = -1e30

PROJ_ROWS = 512
M_CHUNK = 256
M_STEP_ROWS = 2 * M_CHUNK
FOX_TQ = 512
FOX_UNROLL = 8
AUG_LANES = LANES // F_HEADS
VMEM_LIMIT = 56 * 1024 * 1024


def _dot(a, b):
    return jnp.dot(a, b, preferred_element_type=F32)


def _dot_nt(a, b):
    return lax.dot_general(a, b, (((1,), (1,)), ((), ())), preferred_element_type=F32)


def _silu(x):
    return x / (1.0 + jnp.exp2(x * (-LOG2E)))


def _log_sigmoid(x):
    return jnp.minimum(x, 0.0) - jnp.log1p(jnp.exp(-jnp.abs(x)))


def _adaln_kernel(c_ref, w_ref, b_ref, o_ref):
    c = c_ref[...]
    w = w_ref[...]
    c_hi = c.astype(BF16)
    c_lo = (c - c_hi.astype(F32)).astype(BF16)
    w_hi = w.astype(BF16)
    w_lo = (w - w_hi.astype(F32)).astype(BF16)
    acc = _dot(c_hi, w_hi) + _dot(c_hi, w_lo) + _dot(c_lo, w_hi)
    o_ref[...] = acc + b_ref[...]


def _adaln(c_pad, w_ada, b_ada):
    rows, d = c_pad.shape
    n = w_ada.shape[1]
    tn = d
    return pl.pallas_call(
        _adaln_kernel,
        grid=(n // tn,),
        in_specs=[
            pl.BlockSpec((rows, d), lambda j: (0, 0)),
            pl.BlockSpec((d, tn), lambda j: (0, j)),
            pl.BlockSpec((1, tn), lambda j: (0, j)),
        ],
        out_specs=pl.BlockSpec((rows, tn), lambda j: (0, j)),
        out_shape=jax.ShapeDtypeStruct((rows, n), F32),
        compiler_params=pltpu.CompilerParams(
            dimension_semantics=("arbitrary",), vmem_limit_bytes=VMEM_LIMIT),
        name="adaln",
    )(c_pad, w_ada, b_ada)


def _seg_cumsum(x, seg_row, n):
    shift = 1
    while shift < n:
        rolled = pltpu.roll(x, shift, axis=0)
        x = x + jnp.where(seg_row >= shift, rolled, 0.0)
        shift *= 2
    return x


def _proj_kernel(x_ref, mod_ref, ng_ref, wm_ref, wf_ref, wg_ref, cw_ref, cb_ref, gb_ref, qg_ref, kg_ref,
                 hsum_ref, sel_ref, aug1_ref, u_ref, kt_ref, gc_ref, gtm_ref, qaug_ref, kaug_ref, ext_ref, cum_ref):
    j = pl.program_id(1)
    ts = x_ref.shape[1]

    @pl.when(j == 0)
    def _():
        ext_ref[0:SUBLANES, :] = jnp.zeros((SUBLANES, ext_ref.shape[1]), F32)
        cum_ref[...] = jnp.zeros_like(cum_ref)

    x = x_ref[0]
    ms = jnp.mean(x * x, axis=-1, keepdims=True)
    shift = mod_ref[0, 0:1, :]
    gain = ng_ref[...] * (1.0 + mod_ref[0, 1:2, :])
    h = (x * lax.rsqrt(ms + EPS) * gain + shift).astype(BF16)

    def group(g):
        w_ref, first = (wm_ref, W_MQ) if g < W_FQ else (wf_ref, W_FQ)
        return _dot(h, w_ref[:, (g - first) * GROUP_W:(g - first + 1) * GROUP_W])

    def store(slot, val):
        u_ref[0, :, slot * GROUP_W:(slot + 1) * GROUP_W] = val.astype(BF16)

    ug = _dot(h, wg_ref[...]) + gb_ref[...]
    lane = lax.broadcasted_iota(jnp.int32, ug.shape, 1)
    row = lax.broadcasted_iota(jnp.int32, ug.shape, 0)
    lf = _log_sigmoid(ug) * LOG2E
    seg_row = jnp.where(lane < 2 * M_HEADS, row & (M_CHUNK - 1), row)
    cs = _seg_cumsum(lf, seg_row, ts)
    cs = cs + jnp.where(lane >= 2 * M_HEADS, cum_ref[0:1, :], 0.0)
    cum_ref[0:1, :] = cs[ts - 1:ts, :]
    gc = jnp.where(lane < M_HEADS, ug * LOG2E, cs)
    gc_ref[0] = gc
    gtm_ref[0] = gc.T[0:2 * M_HEADS, :]

    for g, slot, gain_ref in ((W_FQ, U_FQ, qg_ref), (W_FK, U_FK, kg_ref)):
        u = group(g)
        sq = (u * u).astype(BF16)
        half = hsum_ref.shape[0]
        ssq = jnp.concatenate([_dot(sq[:, 0:half], hsum_ref[...]), _dot(sq[:, half:], hsum_ref[...])], axis=1)
        store(slot, u * lax.rsqrt(ssq * (1.0 / F_DH) + EPS) * gain_ref[...])

    ext_ref[SUBLANES:, 0:GROUP_W] = group(W_MQ)
    ext_ref[SUBLANES:, GROUP_W:] = group(W_MK)
    conv = cb_ref[...] + cw_ref[0:1, :] * ext_ref[SUBLANES - 3:SUBLANES - 3 + ts, :]
    for t in range(1, CONV_K):
        lo = SUBLANES - 3 + t
        conv = conv + cw_ref[t:t + 1, :] * ext_ref[lo:lo + ts, :]
    ext_ref[0:SUBLANES, :] = ext_ref[ts:ts + SUBLANES, :]
    qk = _silu(conv)
    store(U_MQ, qk[:, 0:GROUP_W])
    kt_ref[0] = (qk[:, GROUP_W:] * (1.0 / math.sqrt(M_DH))).T.astype(BF16)

    for g, slot in ((W_MV, U_MV), (W_MO, U_MO), (W_MZ, U_MZ), (W_FV, U_FV), (W_FZ, U_FZ)):
        store(slot, group(g))

    def split3(a):
        hi = a.astype(BF16)
        r1 = a - hi.astype(F32)
        mid = r1.astype(BF16)
        lo = (r1 - mid.astype(F32)).astype(BF16)
        return jnp.concatenate([hi, mid, lo], axis=1)

    f_blk = jnp.broadcast_to(cs[0:1, :], (SUBLANES, LANES))
    blk_parts = split3(f_blk)
    k_const = _dot(blk_parts, sel_ref[1])[0:1, :] + aug1_ref[0:1, :]
    q_const = _dot(blk_parts, sel_ref[2])[0:1, :] + aug1_ref[1:2, :]
    kaug_ref[0] = (_dot(split3(cs[0:1, :] - cs), sel_ref[0]) + k_const).astype(BF16)
    qaug_ref[0] = jnp.broadcast_to(q_const, (ts, LANES)).astype(BF16)


def _proj(x, mod3, norm_g, w_m, w_f, w_gate, conv_w, conv_b, gate_b, qg, kg, hsum, sel, aug1):
    b, s, d = x.shape
    ts = PROJ_ROWS
    const = lambda *shape: pl.BlockSpec(shape, lambda bi, j: (0,) * len(shape))
    return pl.pallas_call(
        _proj_kernel,
        grid=(b, s // ts),
        in_specs=[
            pl.BlockSpec((1, ts, d), lambda bi, j: (bi, j, 0)),
            pl.BlockSpec((1, 3, d), lambda bi, j: (bi, 0, 0)),
            const(1, d),
            const(d, w_m.shape[1]),
            const(d, w_f.shape[1]),
            const(d, LANES),
            const(CONV_K, 2 * M_W),
            const(1, 2 * M_W),
            const(1, LANES),
            const(1, F_W),
            const(1, F_W),
            const(F_W // 2, F_W // 2),
            const(3, 3 * LANES, LANES),
            const(2, LANES),
        ],
        out_specs=[
            pl.BlockSpec((1, ts, N_SLOTS * GROUP_W), lambda bi, j: (bi, j, 0)),
            pl.BlockSpec((1, M_W, ts), lambda bi, j: (bi, 0, j)),
            pl.BlockSpec((1, ts, LANES), lambda bi, j: (bi, j, 0)),
            pl.BlockSpec((1, 2 * M_HEADS, ts), lambda bi, j: (bi, 0, j)),
            pl.BlockSpec((1, ts, LANES), lambda bi, j: (bi, j, 0)),
            pl.BlockSpec((1, ts, LANES), lambda bi, j: (bi, j, 0)),
        ],
        out_shape=[
            jax.ShapeDtypeStruct((b, s, N_SLOTS * GROUP_W), BF16),
            jax.ShapeDtypeStruct((b, M_W, s), BF16),
            jax.ShapeDtypeStruct((b, s, LANES), F32),
            jax.ShapeDtypeStruct((b, 2 * M_HEADS, s), F32),
            jax.ShapeDtypeStruct((b, s, LANES), BF16),
            jax.ShapeDtypeStruct((b, s, LANES), BF16),
        ],
        scratch_shapes=[
            pltpu.VMEM((ts + SUBLANES, 2 * M_W), F32),
            pltpu.VMEM((SUBLANES, LANES), F32),
        ],
        compiler_params=pltpu.CompilerParams(
            dimension_semantics=("arbitrary", "arbitrary"), vmem_limit_bytes=VMEM_LIMIT),
        name="proj",
    )(x, mod3, norm_g, w_m, w_f, w_gate, conv_w, conv_b, gate_b, qg, kg, hsum, sel, aug1)


def _mlstm_out_kernel(q_ref, kt_ref, v_ref, o_ref, z_ref, gc_ref, gt_ref, lng_ref, x_ref, mod_ref, yf_ref, w_ref,
                      out_ref, cn_ref, m_ref, ym_ref):
    c = pl.program_id(1)
    L = M_CHUNK

    @pl.when(c == 0)
    def _():
        cn_ref[...] = jnp.zeros_like(cn_ref)
        m_ref[...] = jnp.zeros_like(m_ref)

    rows = lax.broadcasted_iota(jnp.int32, (L, L), 0)
    cols = lax.broadcasted_iota(jnp.int32, (L, L), 1)
    causal = cols <= rows
    ones_blk = jnp.ones((L, M_DH), BF16)

    for ck, hd in [(ck, hd) for ck in range(q_ref.shape[1] // L) for hd in range(M_HEADS)]:
        t = slice(ck * L, (ck + 1) * L)
        sl = slice(hd * M_DH, (hd + 1) * M_DH)
        q = q_ref[0, t, sl]
        kt = kt_ref[0, sl, t]
        v1 = jnp.concatenate([v_ref[0, t, sl], ones_blk], axis=1)
        b_c = jnp.broadcast_to(gc_ref[0, t, M_HEADS + hd:M_HEADS + hd + 1], (L, LANES))
        r_row = gt_ref[0, hd:hd + 1, t] - gt_ref[0, M_HEADS + hd:M_HEADS + hd + 1, t]
        m_prev = m_ref[hd:hd + 1, :]
        cn = cn_ref[hd]

        rmat = jnp.where(causal, jnp.broadcast_to(r_row, (L, L)), NEG_BIG)
        g = jnp.maximum(m_prev, jnp.max(rmat, axis=-1, keepdims=True))
        sm = _dot(q, kt) * jnp.exp2(rmat - jnp.tile(g, (1, L // LANES)))
        w_inter = jnp.exp2(m_prev - g)
        numden = jnp.tile(w_inter, (1, 2)) * _dot(q, cn.astype(BF16)) + _dot(sm.astype(BF16), v1)
        e_neg_m = jnp.exp2(-(b_c + g))
        den = jnp.maximum(jnp.abs(numden[:, M_DH:]), e_neg_m)

        g_last = g[L - 1:L, :]
        ws_row = jnp.exp2(r_row - jnp.tile(g_last, (1, L // LANES)))
        ktw = kt * ws_row.astype(BF16)
        decay = jnp.exp2(m_prev - g_last)
        cn_ref[hd] = jnp.tile(decay, (1, 2)) * cn + _dot(ktw, v1)
        m_ref[hd:hd + 1, :] = b_c[L - 1:L, :] + g_last

        hm = numden[:, 0:M_DH] / (den * (1.0 + jnp.exp2(o_ref[0, t, sl].astype(F32) * (-LOG2E))))
        mu = jnp.mean(hm, axis=-1, keepdims=True)
        dv = hm - mu
        var = jnp.mean(dv * dv, axis=-1, keepdims=True)
        y = dv * lax.rsqrt(var + EPS) * lng_ref[:, sl]
        ym_ref[t, sl] = (y * _silu(z_ref[0, t, sl].astype(F32))).astype(BF16)

    proj = _dot(ym_ref[...], w_ref[0:M_W, :]) + _dot(yf_ref[0], w_ref[M_W:, :])
    out_ref[0] = x_ref[0] + mod_ref[0, 2:3, :] * proj


def _mlstm_out(u, kt, gc, gtm, ln_g, x, mod3, yf, w_out):
    b, s, d = x.shape
    L = M_STEP_ROWS
    col = lambda g: pl.BlockSpec((1, L, GROUP_W), lambda bi, c, g=g: (bi, c, g))
    rows = lambda width: pl.BlockSpec((1, L, width), lambda bi, c: (bi, c, 0))
    return pl.pallas_call(
        _mlstm_out_kernel,
        grid=(b, s // L),
        in_specs=[
            col(U_MQ),
            pl.BlockSpec((1, M_W, L), lambda bi, c: (bi, 0, c)),
            col(U_MV), col(U_MO), col(U_MZ),
            rows(LANES),
            pl.BlockSpec((1, 2 * M_HEADS, L), lambda bi, c: (bi, 0, c)),
            pl.BlockSpec((1, M_W), lambda bi, c: (0, 0)),
            rows(d),
            pl.BlockSpec((1, 3, d), lambda bi, c: (bi, 0, 0)),
            rows(F_W),
            pl.BlockSpec((M_W + F_W, d), lambda bi, c: (0, 0)),
        ],
        out_specs=rows(d),
        out_shape=jax.ShapeDtypeStruct((b, s, d), x.dtype),
        scratch_shapes=[
            pltpu.VMEM((M_HEADS, M_DH, 2 * M_DH), F32),
            pltpu.VMEM((SUBLANES, LANES), F32),
            pltpu.VMEM((L, M_W), BF16),
        ],
        compiler_params=pltpu.CompilerParams(
            dimension_semantics=("arbitrary", "arbitrary"), vmem_limit_bytes=VMEM_LIMIT),
        name="mlstm_out",
    )(u, kt, u, u, u, gc, gtm, ln_g, x, mod3, yf, w_out)


def _fox_kernel(q_ref, k_ref, v_ref, z_ref, qaug_ref, kaug_ref, y_ref, s_ref, mb_ref, al_ref, m_all, accl_all):
    T = FOX_TQ
    nq = q_ref.shape[1] // T
    n_main = nq * (nq - 1) // 2
    U = FOX_UNROLL
    assert U % 2 == 0 and nq % U == 0 and n_main % U == 0 and n_main >= 2 * U
    first_head = lax.broadcasted_iota(jnp.int32, (T, LANES), 1) < F_DH

    def block(ref, blk):
        return ref[0, pl.ds(pl.multiple_of(blk * T, T), T), :]

    lane = lax.broadcasted_iota(jnp.int32, (T, LANES), 1)
    aug_lo = pl.program_id(1) * (2 * AUG_LANES)
    aug_mask = [(lane - (aug_lo + hd * AUG_LANES)).astype(jnp.uint32) < AUG_LANES for hd in range(2)]

    def scores(i, jk, buf, diag):
        q = block(q_ref, i)
        zero = jnp.zeros_like(q)
        q_heads = (jnp.where(first_head, q, zero), jnp.where(first_head, zero, q))
        k = jnp.concatenate([block(k_ref, jk), block(kaug_ref, jk)], axis=1)
        qa = block(qaug_ref, i)
        for hd in range(2):
            qx = jnp.concatenate([q_heads[hd], jnp.where(aug_mask[hd], qa, zero)], axis=1)
            s = _dot_nt(qx, k)
            if diag:
                rows = lax.broadcasted_iota(jnp.int32, (T, T), 0)
                cols = lax.broadcasted_iota(jnp.int32, (T, T), 1)
                s = jnp.where(cols <= rows, s, NEG_BIG)
            row_max = jnp.max(s, axis=-1, keepdims=True)
            if diag:
                m_new = jnp.broadcast_to(row_max, (T, LANES))
            else:
                m_old = m_all[i, hd]
                m_new = jnp.maximum(m_old, row_max)
                al_ref[buf, hd] = jnp.exp2(m_old - m_new)
            m_all[i, hd] = m_new
            mb_ref[buf, hd] = m_new
            s_ref[buf, hd] = s

    def accumulate(i, jk, buf, diag):
        v = block(v_ref, jk)
        one = jnp.ones_like(v)
        v_heads = (jnp.where(first_head, v, one), jnp.where(first_head, one, v))
        for hd in range(2):
            p = jnp.exp2(s_ref[buf, hd] - jnp.tile(mb_ref[buf, hd], (1, T // LANES)))
            pv = _dot(p.astype(BF16), v_heads[hd])
            if diag:
                accl_all[i, hd] = pv
            else:
                accl_all[i, hd] = al_ref[buf, hd] * accl_all[i, hd] + pv

    def advance(i, jk):
        wrap = jk + 1 >= i
        return jnp.where(wrap, i + 1, i), jnp.where(wrap, 0, jk + 1)

    scores(0, 0, 0, True)

    def diag_body(jj, carry):
        t0 = U * jj
        for u in range(U):
            scores(t0 + u + 1, t0 + u + 1, (u + 1) % 2, True)
            accumulate(t0 + u, t0 + u, u % 2, True)
        return carry

    lax.fori_loop(0, nq // U - 1, diag_body, 0)
    for u in range(U):
        t = nq - U + u
        if u < U - 1:
            scores(t + 1, t + 1, (u + 1) % 2, True)
        else:
            scores(1, 0, 0, False)
        accumulate(t, t, u % 2, True)

    def main_body(_, cur):
        for u in range(U):
            nxt = advance(*cur)
            scores(*nxt, (u + 1) % 2, False)
            accumulate(*cur, u % 2, False)
            cur = nxt
        return cur

    cur = lax.fori_loop(0, n_main // U - 1, main_body, (jnp.int32(1), jnp.int32(0)))
    for u in range(U):
        nxt = advance(*cur)
        if u < U - 1:
            scores(*nxt, (u + 1) % 2, False)
        accumulate(*cur, u % 2, False)
        cur = nxt

    def finish(i, carry):
        outs = []
        for hd in range(2):
            a = accl_all[i, hd]
            outs.append(a * pltpu.roll(1.0 / a, F_DH, axis=1))
        out = jnp.where(first_head, outs[0], outs[1])
        rows = pl.ds(pl.multiple_of(i * T, T), T)
        y_ref[0, rows, :] = (out * _silu(z_ref[0, rows, :].astype(F32))).astype(BF16)
        return carry

    lax.fori_loop(0, nq, finish, 0)


def _fox(u, qaug, kaug):
    b, s, _ = u.shape
    T = FOX_TQ
    nq = s // T
    blocks_per_group = GROUP_W // LANES
    qb, kb, vb, zb = (slot * blocks_per_group for slot in (U_FQ, U_FK, U_FV, U_FZ))
    seq = lambda first: pl.BlockSpec((1, s, LANES), lambda bi, p: (bi, 0, first + p))
    return pl.pallas_call(
        _fox_kernel,
        grid=(b, F_HEADS // 2),
        in_specs=[
            seq(qb), seq(kb), seq(vb), seq(zb),
            pl.BlockSpec((1, s, LANES), lambda bi, p: (bi, 0, 0)),
            pl.BlockSpec((1, s, LANES), lambda bi, p: (bi, 0, 0)),
        ],
        out_specs=seq(0),
        out_shape=jax.ShapeDtypeStruct((b, s, F_W), BF16),
        scratch_shapes=[
            pltpu.VMEM((2, 2, T, T), F32),
            pltpu.VMEM((2, 2, T, LANES), F32),
            pltpu.VMEM((2, 2, T, LANES), F32),
            pltpu.VMEM((nq, 2, T, LANES), F32),
            pltpu.VMEM((nq, 2, T, LANES), F32),
        ],
        compiler_params=pltpu.CompilerParams(
            dimension_semantics=("arbitrary", "arbitrary"), vmem_limit_bytes=VMEM_LIMIT),
        name="fox",
    )(u, u, u, u, qaug, kaug)


def _bias_column_constants():
    sel = np.zeros((3, 3 * LANES, LANES), np.float32)
    aug1 = np.zeros((2, LANES), np.float32)
    for h in range(F_HEADS):
        src_lane = 2 * M_HEADS + h
        base = h * AUG_LANES
        for c in range(3):
            sel[0, c * LANES + src_lane, base + c] = 1.0
            sel[1, c * LANES + src_lane, base + 6 + c] = -1.0
            sel[2, c * LANES + src_lane, base + 3 + c] = 1.0
            aug1[0, base + 3 + c] = 1.0
            aug1[1, base + c] = 1.0
            aug1[1, base + 6 + c] = 1.0
    return jnp.asarray(sel, BF16), jnp.asarray(aug1, F32)


def _layer(x, c_pad, norm_g, w_ada, b_ada, w_in, conv_w, conv_b, b_igate, b_fgate_m,
           mlstm_norm_g, b_fgate_f, fox_qnorm_g, fox_knorm_g, w_out):
    b, s, d = x.shape
    mod = _adaln(c_pad, w_ada, b_ada[None, :])
    mod3 = mod[:b].reshape(b, 3, d)

    n_m = 5 * M_W
    n_f = 4 * F_W
    w_m = w_in[:, :n_m].astype(BF16)
    w_f = w_in[:, n_m + 2 * M_HEADS:n_m + 2 * M_HEADS + n_f].astype(BF16)
    w_gate = jnp.concatenate(
        [w_in[:, n_m:n_m + 2 * M_HEADS], w_in[:, n_m + 2 * M_HEADS + n_f:]], axis=1)
    n_gate = w_gate.shape[1]
    w_gate = jnp.pad(w_gate, ((0, 0), (0, LANES - n_gate))).astype(BF16)
    gate_b = jnp.pad(jnp.concatenate([b_igate, b_fgate_m, b_fgate_f]), (0, LANES - n_gate))[None, :]
    qg = jnp.tile(fox_qnorm_g, F_HEADS)[None, :] * (LOG2E / math.sqrt(F_DH))
    kg = jnp.tile(fox_knorm_g, F_HEADS)[None, :]
    head_id = jnp.arange(F_W // 2) // F_DH
    hsum = (head_id[:, None] == head_id[None, :]).astype(BF16)

    assert PROJ_ROWS == FOX_TQ
    sel, aug1 = _bias_column_constants()
    u, kt, gc, gtm, qaug, kaug = _proj(x, mod3, norm_g[None, :], w_m, w_f, w_gate, conv_w, conv_b[None, :],
                                       gate_b, qg, kg, hsum, sel, aug1)
    yf = _fox(u, qaug, kaug)
    return _mlstm_out(u, kt, gc, gtm, mlstm_norm_g[None, :], x, mod3, yf, w_out.astype(BF16))


def kernel(x, c, norm_g, w_ada, b_ada, w_in, conv_w, conv_b, b_igate, b_fgate_m, mlstm_norm_g,
           b_fgate_f, fox_qnorm_g, fox_knorm_g, w_out):
    depth = norm_g.shape[0]
    b = x.shape[0]
    c_pad = jnp.pad(c, ((0, (-b) % SUBLANES), (0, 0)))
    params = (norm_g, w_ada, b_ada, w_in, conv_w, conv_b, b_igate, b_fgate_m, mlstm_norm_g, b_fgate_f,
              fox_qnorm_g, fox_knorm_g, w_out)
    for l in range(depth):
        layer = [p.reshape(p.shape[1:]) if depth == 1 else p[l] for p in params]
        x = _layer(x, c_pad, *layer)
    return x
```

```python
import math

import jax
import jax.numpy as jnp
import numpy as np
from jax import lax
from jax.experimental import pallas as pl
from jax.experimental.pallas import tpu as pltpu

F32 = jnp.float32
BF16 = jnp.bfloat16

EPS = 1e-6
M_HEADS = 4
M_DH = 128
F_HEADS = 8
F_DH = 64
CONV_K = 4
M_W = M_HEADS * M_DH
F_W = F_HEADS * F_DH
GROUP_W = 512
W_MQ, W_MK, W_MV, W_MO, W_MZ, W_FQ, W_FK, W_FV, W_FZ = range(9)
U_MQ, U_MV, U_MO, U_MZ, U_FQ, U_FK, U_FV, U_FZ = range(8)
N_SLOTS = 8
LANES = 128
SUBLANES = 8
LOG2E = 1.4426950408889634
NEG_BIG = -1e30

PROJ_ROWS = 512
M_CHUNK = 256
M_STEP_ROWS = 2 * M_CHUNK
FOX_TQ = 512
FOX_UNROLL = 8
AUG_LANES = LANES // F_HEADS
VMEM_LIMIT = 56 * 1024 * 1024


def _dot(a, b):
    return jnp.dot(a, b, preferred_element_type=F32)


def _dot_nt(a, b):
    return lax.dot_general(a, b, (((1,), (1,)), ((), ())), preferred_element_type=F32)


def _silu(x):
    return x / (1.0 + jnp.exp2(x * (-LOG2E)))


def _log_sigmoid(x):
    return jnp.minimum(x, 0.0) - jnp.log1p(jnp.exp(-jnp.abs(x)))


def _adaln_kernel(c_ref, w_ref, b_ref, o_ref):
    c = c_ref[...]
    w = w_ref[...]
    c_hi = c.astype(BF16)
    c_lo = (c - c_hi.astype(F32)).astype(BF16)
    w_hi = w.astype(BF16)
    w_lo = (w - w_hi.astype(F32)).astype(BF16)
    acc = _dot(c_hi, w_hi) + _dot(c_hi, w_lo) + _dot(c_lo, w_hi)
    o_ref[...] = acc + b_ref[...]


def _adaln(c_pad, w_ada, b_ada):
    rows, d = c_pad.shape
    n = w_ada.shape[1]
    tn = d
    return pl.pallas_call(
        _adaln_kernel,
        grid=(n // tn,),
        in_specs=[
            pl.BlockSpec((rows, d), lambda j: (0, 0)),
            pl.BlockSpec((d, tn), lambda j: (0, j)),
            pl.BlockSpec((1, tn), lambda j: (0, j)),
        ],
        out_specs=pl.BlockSpec((rows, tn), lambda j: (0, j)),
        out_shape=jax.ShapeDtypeStruct((rows, n), F32),
        compiler_params=pltpu.CompilerParams(
            dimension_semantics=("arbitrary",), vmem_limit_bytes=VMEM_LIMIT),
        name="adaln",
    )(c_pad, w_ada, b_ada)


def _seg_cumsum(x, seg_row, n):
    shift = 1
    while shift < n:
        rolled = pltpu.roll(x, shift, axis=0)
        x = x + jnp.where(seg_row >= shift, rolled, 0.0)
        shift *= 2
    return x


def _proj_kernel(x_ref, mod_ref, ng_ref, wm_ref, wf_ref, wg_ref, cw_ref, cb_ref, gb_ref, qg_ref, kg_ref,
                 hsum_ref, sel_ref, aug1_ref, u_ref, kt_ref, gc_ref, gtm_ref, qaug_ref, kaug_ref, ext_ref, cum_ref):
    j = pl.program_id(1)
    ts = x_ref.shape[1]

    @pl.when(j == 0)
    def _():
        ext_ref[0:SUBLANES, :] = jnp.zeros((SUBLANES, ext_ref.shape[1]), F32)
        cum_ref[...] = jnp.zeros_like(cum_ref)

    x = x_ref[0]
    ms = jnp.mean(x * x, axis=-1, keepdims=True)
    shift = mod_ref[0, 0:1, :]
    gain = ng_ref[...] * (1.0 + mod_ref[0, 1:2, :])
    h = (x * lax.rsqrt(ms + EPS) * gain + shift).astype(BF16)

    def group(g):
        w_ref, first = (wm_ref, W_MQ) if g < W_FQ else (wf_ref, W_FQ)
        return _dot(h, w_ref[:, (g - first) * GROUP_W:(g - first + 1) * GROUP_W])

    def store(slot, val):
        u_ref[0, :, slot * GROUP_W:(slot + 1) * GROUP_W] = val.astype(BF16)

    ug = _dot(h, wg_ref[...]) + gb_ref[...]
    lane = lax.broadcasted_iota(jnp.int32, ug.shape, 1)
    row = lax.broadcasted_iota(jnp.int32, ug.shape, 0)
    lf = _log_sigmoid(ug) * LOG2E
    seg_row = jnp.where(lane < 2 * M_HEADS, row & (M_CHUNK - 1), row)
    cs = _seg_cumsum(lf, seg_row, ts)
    cs = cs + jnp.where(lane >= 2 * M_HEADS, cum_ref[0:1, :], 0.0)
    cum_ref[0:1, :] = cs[ts - 1:ts, :]
    gc = jnp.where(lane < M_HEADS, ug * LOG2E, cs)
    gc_ref[0] = gc
    gtm_ref[0] = gc.T[0:2 * M_HEADS, :]

    for g, slot, gain_ref in ((W_FQ, U_FQ, qg_ref), (W_FK, U_FK, kg_ref)):
        u = group(g)
        sq = (u * u).astype(BF16)
        half = hsum_ref.shape[0]
        ssq = jnp.concatenate([_dot(sq[:, 0:half], hsum_ref[...]), _dot(sq[:, half:], hsum_ref[...])], axis=1)
        store(slot, u * lax.rsqrt(ssq * (1.0 / F_DH) + EPS) * gain_ref[...])

    ext_ref[SUBLANES:, 0:GROUP_W] = group(W_MQ)
    ext_ref[SUBLANES:, GROUP_W:] = group(W_MK)
    conv = cb_ref[...] + cw_ref[0:1, :] * ext_ref[SUBLANES - 3:SUBLANES - 3 + ts, :]
    for t in range(1, CONV_K):
        lo = SUBLANES - 3 + t
        conv = conv + cw_ref[t:t + 1, :] * ext_ref[lo:lo + ts, :]
    ext_ref[0:SUBLANES, :] = ext_ref[ts:ts + SUBLANES, :]
    qk = _silu(conv)
    store(U_MQ, qk[:, 0:GROUP_W])
    kt_ref[0] = (qk[:, GROUP_W:] * (1.0 / math.sqrt(M_DH))).T.astype(BF16)

    for g, slot in ((W_MV, U_MV), (W_MO, U_MO), (W_MZ, U_MZ), (W_FV, U_FV), (W_FZ, U_FZ)):
        store(slot, group(g))

    def split3(a):
        hi = a.astype(BF16)
        r1 = a - hi.astype(F32)
        mid = r1.astype(BF16)
        lo = (r1 - mid.astype(F32)).astype(BF16)
        return jnp.concatenate([hi, mid, lo], axis=1)

    f_blk = jnp.broadcast_to(cs[0:1, :], (SUBLANES, LANES))
    blk_parts = split3(f_blk)
    k_const = _dot(blk_parts, sel_ref[1])[0:1, :] + aug1_ref[0:1, :]
    q_const = _dot(blk_parts, sel_ref[2])[0:1, :] + aug1_ref[1:2, :]
    kaug_ref[0] = (_dot(split3(cs[0:1, :] - cs), sel_ref[0]) + k_const).astype(BF16)
    qaug_ref[0] = jnp.broadcast_to(q_const, (ts, LANES)).astype(BF16)


def _proj(x, mod3, norm_g, w_m, w_f, w_gate, conv_w, conv_b, gate_b, qg, kg, hsum, sel, aug1):
    b, s, d = x.shape
    ts = PROJ_ROWS
    const = lambda *shape: pl.BlockSpec(shape, lambda bi, j: (0,) * len(shape))
    return pl.pallas_call(
        _proj_kernel,
        grid=(b, s // ts),
        in_specs=[
            pl.BlockSpec((1, ts, d), lambda bi, j: (bi, j, 0)),
            pl.BlockSpec((1, 3, d), lambda bi, j: (bi, 0, 0)),
            const(1, d),
            const(d, w_m.shape[1]),
            const(d, w_f.shape[1]),
            const(d, LANES),
            const(CONV_K, 2 * M_W),
            const(1, 2 * M_W),
            const(1, LANES),
            const(1, F_W),
            const(1, F_W),
            const(F_W // 2, F_W // 2),
            const(3, 3 * LANES, LANES),
            const(2, LANES),
        ],
        out_specs=[
            pl.BlockSpec((1, ts, N_SLOTS * GROUP_W), lambda bi, j: (bi, j, 0)),
            pl.BlockSpec((1, M_W, ts), lambda bi, j: (bi, 0, j)),
            pl.BlockSpec((1, ts, LANES), lambda bi, j: (bi, j, 0)),
            pl.BlockSpec((1, 2 * M_HEADS, ts), lambda bi, j: (bi, 0, j)),
            pl.BlockSpec((1, ts, LANES), lambda bi, j: (bi, j, 0)),
            pl.BlockSpec((1, ts, LANES), lambda bi, j: (bi, j, 0)),
        ],
        out_shape=[
            jax.ShapeDtypeStruct((b, s, N_SLOTS * GROUP_W), BF16),
            jax.ShapeDtypeStruct((b, M_W, s), BF16),
            jax.ShapeDtypeStruct((b, s, LANES), F32),
            jax.ShapeDtypeStruct((b, 2 * M_HEADS, s), F32),
            jax.ShapeDtypeStruct((b, s, LANES), BF16),
            jax.ShapeDtypeStruct((b, s, LANES), BF16),
        ],
        scratch_shapes=[
            pltpu.VMEM((ts + SUBLANES, 2 * M_W), F32),
            pltpu.VMEM((SUBLANES, LANES), F32),
        ],
        compiler_params=pltpu.CompilerParams(
            dimension_semantics=("arbitrary", "arbitrary"), vmem_limit_bytes=VMEM_LIMIT),
        name="proj",
    )(x, mod3, norm_g, w_m, w_f, w_gate, conv_w, conv_b, gate_b, qg, kg, hsum, sel, aug1)


def _mlstm_out_kernel(q_ref, kt_ref, v_ref, o_ref, z_ref, gc_ref, gt_ref, lng_ref, x_ref, mod_ref, yf_ref, w_ref,
                      out_ref, cn_ref, m_ref, ym_ref):
    c = pl.program_id(1)
    L = M_CHUNK

    @pl.when(c == 0)
    def _():
        cn_ref[...] = jnp.zeros_like(cn_ref)
        m_ref[...] = jnp.zeros_like(m_ref)

    rows = lax.broadcasted_iota(jnp.int32, (L, L), 0)
    cols = lax.broadcasted_iota(jnp.int32, (L, L), 1)
    causal = cols <= rows
    ones_blk = jnp.ones((L, M_DH), BF16)

    for ck, hd in [(ck, hd) for ck in range(q_ref.shape[1] // L) for hd in range(M_HEADS)]:
        t = slice(ck * L, (ck + 1) * L)
        sl = slice(hd * M_DH, (hd + 1) * M_DH)
        q = q_ref[0, t, sl]
        kt = kt_ref[0, sl, t]
        v1 = jnp.concatenate([v_ref[0, t, sl], ones_blk], axis=1)
        b_c = jnp.broadcast_to(gc_ref[0, t, M_HEADS + hd:M_HEADS + hd + 1], (L, LANES))
        r_row = gt_ref[0, hd:hd + 1, t] - gt_ref[0, M_HEADS + hd:M_HEADS + hd + 1, t]
        m_prev = m_ref[hd:hd + 1, :]
        cn = cn_ref[hd]

        rmat = jnp.where(causal, jnp.broadcast_to(r_row, (L, L)), NEG_BIG)
        g = jnp.maximum(m_prev, jnp.max(rmat, axis=-1, keepdims=True))
        sm = _dot(q, kt) * jnp.exp2(rmat - jnp.tile(g, (1, L // LANES)))
        w_inter = jnp.exp2(m_prev - g)
        numden = jnp.tile(w_inter, (1, 2)) * _dot(q, cn.astype(BF16)) + _dot(sm.astype(BF16), v1)
        e_neg_m = jnp.exp2(-(b_c + g))
        den = jnp.maximum(jnp.abs(numden[:, M_DH:]), e_neg_m)

        g_last = g[L - 1:L, :]
        ws_row = jnp.exp2(r_row - jnp.tile(g_last, (1, L // LANES)))
        ktw = kt * ws_row.astype(BF16)
        decay = jnp.exp2(m_prev - g_last)
        cn_ref[hd] = jnp.tile(decay, (1, 2)) * cn + _dot(ktw, v1)
        m_ref[hd:hd + 1, :] = b_c[L - 1:L, :] + g_last

        hm = numden[:, 0:M_DH] / (den * (1.0 + jnp.exp2(o_ref[0, t, sl].astype(F32) * (-LOG2E))))
        mu = jnp.mean(hm, axis=-1, keepdims=True)
        dv = hm - mu
        var = jnp.mean(dv * dv, axis=-1, keepdims=True)
        y = dv * lax.rsqrt(var + EPS) * lng_ref[:, sl]
        ym_ref[t, sl] = (y * _silu(z_ref[0, t, sl].astype(F32))).astype(BF16)

    proj = _dot(ym_ref[...], w_ref[0:M_W, :]) + _dot(yf_ref[0], w_ref[M_W:, :])
    out_ref[0] = x_ref[0] + mod_ref[0, 2:3, :] * proj


def _mlstm_out(u, kt, gc, gtm, ln_g, x, mod3, yf, w_out):
    b, s, d = x.shape
    L = M_STEP_ROWS
    col = lambda g: pl.BlockSpec((1, L, GROUP_W), lambda bi, c, g=g: (bi, c, g))
    rows = lambda width: pl.BlockSpec((1, L, width), lambda bi, c: (bi, c, 0))
    return pl.pallas_call(
        _mlstm_out_kernel,
        grid=(b, s // L),
        in_specs=[
            col(U_MQ),
            pl.BlockSpec((1, M_W, L), lambda bi, c: (bi, 0, c)),
            col(U_MV), col(U_MO), col(U_MZ),
            rows(LANES),
            pl.BlockSpec((1, 2 * M_HEADS, L), lambda bi, c: (bi, 0, c)),
            pl.BlockSpec((1, M_W), lambda bi, c: (0, 0)),
            rows(d),
            pl.BlockSpec((1, 3, d), lambda bi, c: (bi, 0, 0)),
            rows(F_W),
            pl.BlockSpec((M_W + F_W, d), lambda bi, c: (0, 0)),
        ],
        out_specs=rows(d),
        out_shape=jax.ShapeDtypeStruct((b, s, d), x.dtype),
        scratch_shapes=[
            pltpu.VMEM((M_HEADS, M_DH, 2 * M_DH), F32),
            pltpu.VMEM((SUBLANES, LANES), F32),
            pltpu.VMEM((L, M_W), BF16),
        ],
        compiler_params=pltpu.CompilerParams(
            dimension_semantics=("arbitrary", "arbitrary"), vmem_limit_bytes=VMEM_LIMIT),
        name="mlstm_out",
    )(u, kt, u, u, u, gc, gtm, ln_g, x, mod3, yf, w_out)


def _fox_kernel(q_ref, k_ref, v_ref, z_ref, qaug_ref, kaug_ref, y_ref, s_ref, mb_ref, al_ref, m_all, accl_all):
    T = FOX_TQ
    H = T // 2
    nq = q_ref.shape[1] // T
    n_main = nq * (nq - 1) // 2
    U = FOX_UNROLL
    assert U % 2 == 0 and nq % U == 0 and n_main % U == 0 and n_main >= 2 * U
    first_head = lax.broadcasted_iota(jnp.int32, (T, LANES), 1) < F_DH

    def block(ref, blk):
        return ref[0, pl.ds(pl.multiple_of(blk * T, T), T), :]

    lane = lax.broadcasted_iota(jnp.int32, (T, LANES), 1)
    aug_lo = pl.program_id(1) * (2 * AUG_LANES)
    aug_mask = [(lane - (aug_lo + hd * AUG_LANES)).astype(jnp.uint32) < AUG_LANES for hd in range(2)]

    def scores(i, jk, buf, diag):
        q = block(q_ref, i)
        zero = jnp.zeros_like(q)
        q_heads = (jnp.where(first_head, q, zero), jnp.where(first_head, zero, q))
        k = jnp.concatenate([block(k_ref, jk), block(kaug_ref, jk)], axis=1)
        qa = block(qaug_ref, i)
        for hd in range(2):
            qx = jnp.concatenate([q_heads[hd], jnp.where(aug_mask[hd], qa, zero)], axis=1)
            if diag:
                for r0, width in ((0, H), (H, T)):
                    rows = lax.broadcasted_iota(jnp.int32, (H, width), 0)
                    cols = lax.broadcasted_iota(jnp.int32, (H, width), 1)
                    s = jnp.where(cols <= rows + r0, _dot_nt(qx[r0:r0 + H], k[0:width]), NEG_BIG)
                    m_new = jnp.broadcast_to(jnp.max(s, axis=-1, keepdims=True), (H, LANES))
                    m_all[i, hd, r0:r0 + H] = m_new
                    mb_ref[buf, hd, r0:r0 + H] = m_new
                    s_ref[buf, hd, r0:r0 + H, 0:width] = s
            else:
                s = _dot_nt(qx, k)
                m_old = m_all[i, hd]
                m_new = jnp.maximum(m_old, jnp.max(s, axis=-1, keepdims=True))
                al_ref[buf, hd] = jnp.exp2(m_old - m_new)
                m_all[i, hd] = m_new
                mb_ref[buf, hd] = m_new
                s_ref[buf, hd] = s

    def accumulate(i, jk, buf, diag):
        v = block(v_ref, jk)
        one = jnp.ones_like(v)
        v_heads = (jnp.where(first_head, v, one), jnp.where(first_head, one, v))
        for hd in range(2):
            if diag:
                for r0, width in ((0, H), (H, T)):
                    m = mb_ref[buf, hd, r0:r0 + H]
                    p = jnp.exp2(s_ref[buf, hd, r0:r0 + H, 0:width] - jnp.tile(m, (1, width // LANES)))
                    accl_all[i, hd, r0:r0 + H] = _dot(p.astype(BF16), v_heads[hd][0:width])
            else:
                p = jnp.exp2(s_ref[buf, hd] - jnp.tile(mb_ref[buf, hd], (1, T // LANES)))
                accl_all[i, hd] = al_ref[buf, hd] * accl_all[i, hd] + _dot(p.astype(BF16), v_heads[hd])

    def advance(i, jk):
        wrap = jk + 1 >= i
        return jnp.where(wrap, i + 1, i), jnp.where(wrap, 0, jk + 1)

    scores(0, 0, 0, True)

    def diag_body(jj, carry):
        t0 = U * jj
        for u in range(U):
            scores(t0 + u + 1, t0 + u + 1, (u + 1) % 2, True)
            accumulate(t0 + u, t0 + u, u % 2, True)
        return carry

    lax.fori_loop(0, nq // U - 1, diag_body, 0)
    for u in range(U):
        t = nq - U + u
        if u < U - 1:
            scores(t + 1, t + 1, (u + 1) % 2, True)
        else:
            scores(1, 0, 0, False)
        accumulate(t, t, u % 2, True)

    def main_body(_, cur):
        for u in range(U):
            nxt = advance(*cur)
            scores(*nxt, (u + 1) % 2, False)
            accumulate(*cur, u % 2, False)
            cur = nxt
        return cur

    cur = lax.fori_loop(0, n_main // U - 1, main_body, (jnp.int32(1), jnp.int32(0)))
    for u in range(U):
        nxt = advance(*cur)
        if u < U - 1:
            scores(*nxt, (u + 1) % 2, False)
        accumulate(*cur, u % 2, False)
        cur = nxt

    def finish(i, carry):
        outs = []
        for hd in range(2):
            a = accl_all[i, hd]
            outs.append(a * pltpu.roll(1.0 / a, F_DH, axis=1))
        out = jnp.where(first_head, outs[0], outs[1])
        rows = pl.ds(pl.multiple_of(i * T, T), T)
        y_ref[0, rows, :] = (out * _silu(z_ref[0, rows, :].astype(F32))).astype(BF16)
        return carry

    lax.fori_loop(0, nq, finish, 0)


def _fox(u, qaug, kaug):
    b, s, _ = u.shape
    T = FOX_TQ
    nq = s // T
    blocks_per_group = GROUP_W // LANES
    qb, kb, vb, zb = (slot * blocks_per_group for slot in (U_FQ, U_FK, U_FV, U_FZ))
    seq = lambda first: pl.BlockSpec((1, s, LANES), lambda bi, p: (bi, 0, first + p))
    return pl.pallas_call(
        _fox_kernel,
        grid=(b, F_HEADS // 2),
        in_specs=[
            seq(qb), seq(kb), seq(vb), seq(zb),
            pl.BlockSpec((1, s, LANES), lambda bi, p: (bi, 0, 0)),
            pl.BlockSpec((1, s, LANES), lambda bi, p: (bi, 0, 0)),
        ],
        out_specs=seq(0),
        out_shape=jax.ShapeDtypeStruct((b, s, F_W), BF16),
        scratch_shapes=[
            pltpu.VMEM((2, 2, T, T), F32),
            pltpu.VMEM((2, 2, T, LANES), F32),
            pltpu.VMEM((2, 2, T, LANES), F32),
            pltpu.VMEM((nq, 2, T, LANES), F32),
            pltpu.VMEM((nq, 2, T, LANES), F32),
        ],
        compiler_params=pltpu.CompilerParams(
            dimension_semantics=("arbitrary", "arbitrary"), vmem_limit_bytes=VMEM_LIMIT),
        name="fox",
    )(u, u, u, u, qaug, kaug)


W_PREP_ROWS = 128


def _w_prep_kernel(w_ref, wm_ref, wf_ref, wg_ref):
    n_m, n_f, n_g = wm_ref.shape[1], wf_ref.shape[1], 2 * M_HEADS
    wm_ref[...] = w_ref[:, 0:n_m].astype(BF16)
    wf_ref[...] = w_ref[:, n_m + n_g:n_m + n_g + n_f].astype(BF16)
    a = w_ref[:, n_m:n_m + LANES]
    b = w_ref[:, n_m + n_f:n_m + n_f + 2 * n_g]
    b = jnp.concatenate([b, jnp.zeros((b.shape[0], LANES - 2 * n_g), F32)], axis=1)
    lane = lax.broadcasted_iota(jnp.int32, a.shape, 1)
    wg_ref[...] = jnp.where(lane < n_g, a, jnp.where(lane < 2 * n_g, b, 0.0)).astype(BF16)


def _w_prep(w_in):
    d, n = w_in.shape
    n_m, n_f = 5 * M_W, 4 * F_W
    assert n == n_m + n_f + 2 * 2 * M_HEADS and F_HEADS == 2 * M_HEADS
    rows = lambda width: pl.BlockSpec((W_PREP_ROWS, width), lambda i: (i, 0))
    return pl.pallas_call(
        _w_prep_kernel,
        grid=(d // W_PREP_ROWS,),
        in_specs=[rows(n)],
        out_specs=[rows(n_m), rows(n_f), rows(LANES)],
        out_shape=[jax.ShapeDtypeStruct((d, n_m), BF16), jax.ShapeDtypeStruct((d, n_f), BF16),
                   jax.ShapeDtypeStruct((d, LANES), BF16)],
        compiler_params=pltpu.CompilerParams(
            dimension_semantics=("arbitrary",), vmem_limit_bytes=VMEM_LIMIT),
        name="w_prep",
    )(w_in)


def _bias_column_constants():
    sel = np.zeros((3, 3 * LANES, LANES), np.float32)
    aug1 = np.zeros((2, LANES), np.float32)
    for h in range(F_HEADS):
        src_lane = 2 * M_HEADS + h
        base = h * AUG_LANES
        for c in range(3):
            sel[0, c * LANES + src_lane, base + c] = 1.0
            sel[1, c * LANES + src_lane, base + 6 + c] = -1.0
            sel[2, c * LANES + src_lane, base + 3 + c] = 1.0
            aug1[0, base + 3 + c] = 1.0
            aug1[1, base + c] = 1.0
            aug1[1, base + 6 + c] = 1.0
    return jnp.asarray(sel, BF16), jnp.asarray(aug1, F32)


def _layer(x, c_pad, norm_g, w_ada, b_ada, w_in, conv_w, conv_b, b_igate, b_fgate_m,
           mlstm_norm_g, b_fgate_f, fox_qnorm_g, fox_knorm_g, w_out):
    b, s, d = x.shape
    mod = _adaln(c_pad, w_ada, b_ada[None, :])
    mod3 = mod[:b].reshape(b, 3, d)

    w_m, w_f, w_gate = _w_prep(w_in)
    n_gate = 2 * M_HEADS + F_HEADS
    gate_b = jnp.pad(jnp.concatenate([b_igate, b_fgate_m, b_fgate_f]), (0, LANES - n_gate))[None, :]
    qg = jnp.tile(fox_qnorm_g, F_HEADS)[None, :] * (LOG2E / math.sqrt(F_DH))
    kg = jnp.tile(fox_knorm_g, F_HEADS)[None, :]
    head_id = jnp.arange(F_W // 2) // F_DH
    hsum = (head_id[:, None] == head_id[None, :]).astype(BF16)

    assert PROJ_ROWS == FOX_TQ
    sel, aug1 = _bias_column_constants()
    u, kt, gc, gtm, qaug, kaug = _proj(x, mod3, norm_g[None, :], w_m, w_f, w_gate, conv_w, conv_b[None, :],
                                       gate_b, qg, kg, hsum, sel, aug1)
    yf = _fox(u, qaug, kaug)
    return _mlstm_out(u, kt, gc, gtm, mlstm_norm_g[None, :], x, mod3, yf, w_out.astype(BF16))


def kernel(x, c, norm_g, w_ada, b_ada, w_in, conv_w, conv_b, b_igate, b_fgate_m, mlstm_norm_g,
           b_fgate_f, fox_qnorm_g, fox_knorm_g, w_out):
    depth = norm_g.shape[0]
    b = x.shape[0]
    c_pad = jnp.pad(c, ((0, (-b) % SUBLANES), (0, 0)))
    params = (norm_g, w_ada, b_ada, w_in, conv_w, conv_b, b_igate, b_fgate_m, mlstm_norm_g, b_fgate_f,
              fox_qnorm_g, fox_knorm_g, w_out)
    for l in range(depth):
        layer = [p.reshape(p.shape[1:]) if depth == 1 else p[l] for p in params]
        x = _layer(x, c_pad, *layer)
    return x
```

```python
import math

import jax
import jax.numpy as jnp
import numpy as np
from jax import lax
from jax.experimental import pallas as pl
from jax.experimental.pallas import tpu as pltpu

F32 = jnp.float32
BF16 = jnp.bfloat16

EPS = 1e-6
M_HEADS = 4
M_DH = 128
F_HEADS = 8
F_DH = 64
CONV_K = 4
M_W = M_HEADS * M_DH
F_W = F_HEADS * F_DH
GROUP_W = 512
W_MQ, W_MK, W_MV, W_MO, W_MZ, W_FQ, W_FK, W_FV, W_FZ = range(9)
U_MQ, U_MV, U_MO, U_MZ, U_FQ, U_FK, U_FV, U_FZ = range(8)
N_SLOTS = 8
LANES = 128
SUBLANES = 8
LOG2E = 1.4426950408889634
NEG_BIG = -1e30

PROJ_ROWS = 512
M_CHUNK = 256
M_STEP_ROWS = 2 * M_CHUNK
FOX_TQ = 512
FOX_UNROLL = 8
AUG_LANES = LANES // F_HEADS
VMEM_LIMIT = 56 * 1024 * 1024


def _dot(a, b):
    return jnp.dot(a, b, preferred_element_type=F32)


def _dot_nt(a, b):
    return lax.dot_general(a, b, (((1,), (1,)), ((), ())), preferred_element_type=F32)


def _silu(x):
    return x / (1.0 + jnp.exp2(x * (-LOG2E)))


def _log_sigmoid(x):
    return jnp.minimum(x, 0.0) - jnp.log1p(jnp.exp(-jnp.abs(x)))


def _adaln_kernel(c_ref, w_ref, b_ref, o_ref):
    c = c_ref[...]
    w = w_ref[...]
    c_hi = c.astype(BF16)
    c_lo = (c - c_hi.astype(F32)).astype(BF16)
    w_hi = w.astype(BF16)
    w_lo = (w - w_hi.astype(F32)).astype(BF16)
    acc = _dot(c_hi, w_hi) + _dot(c_hi, w_lo) + _dot(c_lo, w_hi)
    o_ref[...] = acc + b_ref[...]


def _adaln(c_pad, w_ada, b_ada):
    rows, d = c_pad.shape
    n = w_ada.shape[1]
    tn = d
    return pl.pallas_call(
        _adaln_kernel,
        grid=(n // tn,),
        in_specs=[
            pl.BlockSpec((rows, d), lambda j: (0, 0)),
            pl.BlockSpec((d, tn), lambda j: (0, j)),
            pl.BlockSpec((1, tn), lambda j: (0, j)),
        ],
        out_specs=pl.BlockSpec((rows, tn), lambda j: (0, j)),
        out_shape=jax.ShapeDtypeStruct((rows, n), F32),
        compiler_params=pltpu.CompilerParams(
            dimension_semantics=("arbitrary",), vmem_limit_bytes=VMEM_LIMIT),
        name="adaln",
    )(c_pad, w_ada, b_ada)


def _seg_cumsum(x, seg_row, n):
    shift = 1
    while shift < n:
        rolled = pltpu.roll(x, shift, axis=0)
        x = x + jnp.where(seg_row >= shift, rolled, 0.0)
        shift *= 2
    return x


def _proj_kernel(x_ref, mod_ref, ng_ref, wm_ref, wf_ref, wg_ref, cw_ref, cb_ref, gb_ref, qg_ref, kg_ref,
                 hsum_ref, sel_ref, aug1_ref, u_ref, kt_ref, gc_ref, gtm_ref, qaug_ref, kaug_ref, ext_ref, cum_ref):
    j = pl.program_id(1)
    ts = x_ref.shape[1]

    @pl.when(j == 0)
    def _():
        ext_ref[0:SUBLANES, :] = jnp.zeros((SUBLANES, ext_ref.shape[1]), F32)
        cum_ref[...] = jnp.zeros_like(cum_ref)

    x = x_ref[0]
    ms = jnp.mean(x * x, axis=-1, keepdims=True)
    shift = mod_ref[0, 0:1, :]
    gain = ng_ref[...] * (1.0 + mod_ref[0, 1:2, :])
    h = (x * lax.rsqrt(ms + EPS) * gain + shift).astype(BF16)

    def group(g):
        w_ref, first = (wm_ref, W_MQ) if g < W_FQ else (wf_ref, W_FQ)
        return _dot(h, w_ref[:, (g - first) * GROUP_W:(g - first + 1) * GROUP_W])

    def store(slot, val):
        u_ref[0, :, slot * GROUP_W:(slot + 1) * GROUP_W] = val.astype(BF16)

    ug = _dot(h, wg_ref[...]) + gb_ref[...]
    lane = lax.broadcasted_iota(jnp.int32, ug.shape, 1)
    row = lax.broadcasted_iota(jnp.int32, ug.shape, 0)
    lf = _log_sigmoid(ug) * LOG2E
    seg_row = jnp.where(lane < 2 * M_HEADS, row & (M_CHUNK - 1), row)
    cs = _seg_cumsum(lf, seg_row, ts)
    cs = cs + jnp.where(lane >= 2 * M_HEADS, cum_ref[0:1, :], 0.0)
    cum_ref[0:1, :] = cs[ts - 1:ts, :]
    gc = jnp.where(lane < M_HEADS, ug * LOG2E, cs)
    gc_ref[0] = gc
    gtm_ref[0] = gc.T[0:2 * M_HEADS, :]

    for g, slot, gain_ref in ((W_FQ, U_FQ, qg_ref), (W_FK, U_FK, kg_ref)):
        u = group(g)
        sq = (u * u).astype(BF16)
        half = hsum_ref.shape[0]
        ssq = jnp.concatenate([_dot(sq[:, 0:half], hsum_ref[...]), _dot(sq[:, half:], hsum_ref[...])], axis=1)
        store(slot, u * lax.rsqrt(ssq * (1.0 / F_DH) + EPS) * gain_ref[...])

    ext_ref[SUBLANES:, 0:GROUP_W] = group(W_MQ)
    ext_ref[SUBLANES:, GROUP_W:] = group(W_MK)
    conv = cb_ref[...] + cw_ref[0:1, :] * ext_ref[SUBLANES - 3:SUBLANES - 3 + ts, :]
    for t in range(1, CONV_K):
        lo = SUBLANES - 3 + t
        conv = conv + cw_ref[t:t + 1, :] * ext_ref[lo:lo + ts, :]
    ext_ref[0:SUBLANES, :] = ext_ref[ts:ts + SUBLANES, :]
    qk = _silu(conv)
    store(U_MQ, qk[:, 0:GROUP_W])
    kt_ref[0] = (qk[:, GROUP_W:] * (1.0 / math.sqrt(M_DH))).T.astype(BF16)

    for g, slot in ((W_MV, U_MV), (W_MO, U_MO), (W_MZ, U_MZ), (W_FV, U_FV), (W_FZ, U_FZ)):
        store(slot, group(g))

    def split3(a):
        hi = a.astype(BF16)
        r1 = a - hi.astype(F32)
        mid = r1.astype(BF16)
        lo = (r1 - mid.astype(F32)).astype(BF16)
        return jnp.concatenate([hi, mid, lo], axis=1)

    f_blk = jnp.broadcast_to(cs[0:1, :], (SUBLANES, LANES))
    blk_parts = split3(f_blk)
    k_const = _dot(blk_parts, sel_ref[1])[0:1, :] + aug1_ref[0:1, :]
    q_const = _dot(blk_parts, sel_ref[2])[0:1, :] + aug1_ref[1:2, :]
    kaug_ref[0] = (_dot(split3(cs[0:1, :] - cs), sel_ref[0]) + k_const).astype(BF16)
    qaug_ref[0] = jnp.broadcast_to(q_const, (ts, LANES)).astype(BF16)


def _proj(x, mod3, norm_g, w_m, w_f, w_gate, conv_w, conv_b, gate_b, qg, kg, hsum, sel, aug1):
    b, s, d = x.shape
    ts = PROJ_ROWS
    const = lambda *shape: pl.BlockSpec(shape, lambda bi, j: (0,) * len(shape))
    return pl.pallas_call(
        _proj_kernel,
        grid=(b, s // ts),
        in_specs=[
            pl.BlockSpec((1, ts, d), lambda bi, j: (bi, j, 0)),
            pl.BlockSpec((1, 3, d), lambda bi, j: (bi, 0, 0)),
            const(1, d),
            const(d, w_m.shape[1]),
            const(d, w_f.shape[1]),
            const(d, LANES),
            const(CONV_K, 2 * M_W),
            const(1, 2 * M_W),
            const(1, LANES),
            const(1, F_W),
            const(1, F_W),
            const(F_W // 2, F_W // 2),
            const(3, 3 * LANES, LANES),
            const(2, LANES),
        ],
        out_specs=[
            pl.BlockSpec((1, ts, N_SLOTS * GROUP_W), lambda bi, j: (bi, j, 0)),
            pl.BlockSpec((1, M_W, ts), lambda bi, j: (bi, 0, j)),
            pl.BlockSpec((1, ts, LANES), lambda bi, j: (bi, j, 0)),
            pl.BlockSpec((1, 2 * M_HEADS, ts), lambda bi, j: (bi, 0, j)),
            pl.BlockSpec((1, ts, LANES), lambda bi, j: (bi, j, 0)),
            pl.BlockSpec((1, ts, LANES), lambda bi, j: (bi, j, 0)),
        ],
        out_shape=[
            jax.ShapeDtypeStruct((b, s, N_SLOTS * GROUP_W), BF16),
            jax.ShapeDtypeStruct((b, M_W, s), BF16),
            jax.ShapeDtypeStruct((b, s, LANES), F32),
            jax.ShapeDtypeStruct((b, 2 * M_HEADS, s), F32),
            jax.ShapeDtypeStruct((b, s, LANES), BF16),
            jax.ShapeDtypeStruct((b, s, LANES), BF16),
        ],
        scratch_shapes=[
            pltpu.VMEM((ts + SUBLANES, 2 * M_W), F32),
            pltpu.VMEM((SUBLANES, LANES), F32),
        ],
        compiler_params=pltpu.CompilerParams(
            dimension_semantics=("arbitrary", "arbitrary"), vmem_limit_bytes=VMEM_LIMIT),
        name="proj",
    )(x, mod3, norm_g, w_m, w_f, w_gate, conv_w, conv_b, gate_b, qg, kg, hsum, sel, aug1)


def _mlstm_out_kernel(q_ref, kt_ref, v_ref, o_ref, z_ref, gc_ref, gt_ref, lng_ref, x_ref, mod_ref, yf_ref, w_ref,
                      out_ref, cn_ref, m_ref, ym_ref):
    c = pl.program_id(1)
    L = M_CHUNK

    @pl.when(c == 0)
    def _():
        cn_ref[...] = jnp.zeros_like(cn_ref)
        m_ref[...] = jnp.zeros_like(m_ref)

    rows = lax.broadcasted_iota(jnp.int32, (L, L), 0)
    cols = lax.broadcasted_iota(jnp.int32, (L, L), 1)
    causal = cols <= rows
    ones_blk = jnp.ones((L, M_DH), BF16)

    for ck, hd in [(ck, hd) for ck in range(q_ref.shape[1] // L) for hd in range(M_HEADS)]:
        t = slice(ck * L, (ck + 1) * L)
        sl = slice(hd * M_DH, (hd + 1) * M_DH)
        q = q_ref[0, t, sl]
        kt = kt_ref[0, sl, t]
        v1 = jnp.concatenate([v_ref[0, t, sl], ones_blk], axis=1)
        b_c = jnp.broadcast_to(gc_ref[0, t, M_HEADS + hd:M_HEADS + hd + 1], (L, LANES))
        r_row = gt_ref[0, hd:hd + 1, t] - gt_ref[0, M_HEADS + hd:M_HEADS + hd + 1, t]
        m_prev = m_ref[hd:hd + 1, :]
        cn = cn_ref[hd]

        rmat = jnp.where(causal, jnp.broadcast_to(r_row, (L, L)), NEG_BIG)
        g = jnp.maximum(m_prev, jnp.max(rmat, axis=-1, keepdims=True))
        sm = _dot(q, kt) * jnp.exp2(rmat - jnp.tile(g, (1, L // LANES)))
        w_inter = jnp.exp2(m_prev - g)
        numden = jnp.tile(w_inter, (1, 2)) * _dot(q, cn.astype(BF16)) + _dot(sm.astype(BF16), v1)
        e_neg_m = jnp.exp2(-(b_c + g))
        den = jnp.maximum(jnp.abs(numden[:, M_DH:]), e_neg_m)

        g_last = g[L - 1:L, :]
        ws_row = jnp.exp2(r_row - jnp.tile(g_last, (1, L // LANES)))
        ktw = kt * ws_row.astype(BF16)
        decay = jnp.exp2(m_prev - g_last)
        cn_ref[hd] = jnp.tile(decay, (1, 2)) * cn + _dot(ktw, v1)
        m_ref[hd:hd + 1, :] = b_c[L - 1:L, :] + g_last

        hm = numden[:, 0:M_DH] / (den * (1.0 + jnp.exp2(o_ref[0, t, sl].astype(F32) * (-LOG2E))))
        mu = jnp.mean(hm, axis=-1, keepdims=True)
        dv = hm - mu
        var = jnp.mean(dv * dv, axis=-1, keepdims=True)
        y = dv * lax.rsqrt(var + EPS) * lng_ref[:, sl]
        ym_ref[t, sl] = (y * _silu(z_ref[0, t, sl].astype(F32))).astype(BF16)

    proj = _dot(ym_ref[...], w_ref[0:M_W, :]) + _dot(yf_ref[0], w_ref[M_W:, :])
    out_ref[0] = x_ref[0] + mod_ref[0, 2:3, :] * proj


def _mlstm_out(u, kt, gc, gtm, ln_g, x, mod3, yf, w_out):
    b, s, d = x.shape
    L = M_STEP_ROWS
    col = lambda g: pl.BlockSpec((1, L, GROUP_W), lambda bi, c, g=g: (bi, c, g))
    rows = lambda width: pl.BlockSpec((1, L, width), lambda bi, c: (bi, c, 0))
    return pl.pallas_call(
        _mlstm_out_kernel,
        grid=(b, s // L),
        in_specs=[
            col(U_MQ),
            pl.BlockSpec((1, M_W, L), lambda bi, c: (bi, 0, c)),
            col(U_MV), col(U_MO), col(U_MZ),
            rows(LANES),
            pl.BlockSpec((1, 2 * M_HEADS, L), lambda bi, c: (bi, 0, c)),
            pl.BlockSpec((1, M_W), lambda bi, c: (0, 0)),
            rows(d),
            pl.BlockSpec((1, 3, d), lambda bi, c: (bi, 0, 0)),
            rows(F_W),
            pl.BlockSpec((M_W + F_W, d), lambda bi, c: (0, 0)),
        ],
        out_specs=rows(d),
        out_shape=jax.ShapeDtypeStruct((b, s, d), x.dtype),
        scratch_shapes=[
            pltpu.VMEM((M_HEADS, M_DH, 2 * M_DH), F32),
            pltpu.VMEM((SUBLANES, LANES), F32),
            pltpu.VMEM((L, M_W), BF16),
        ],
        compiler_params=pltpu.CompilerParams(
            dimension_semantics=("arbitrary", "arbitrary"), vmem_limit_bytes=VMEM_LIMIT),
        name="mlstm_out",
    )(u, kt, u, u, u, gc, gtm, ln_g, x, mod3, yf, w_out)


def _fox_kernel(q_ref, k_ref, v_ref, z_ref, qaug_ref, kaug_ref, y_ref, s_ref, al_ref, m_all, accl_all):
    T = FOX_TQ
    H = T // 2
    nq = q_ref.shape[1] // T
    n_main = nq * (nq - 1) // 2
    U = FOX_UNROLL
    assert U % 2 == 0 and nq % U == 0 and n_main % U == 0 and n_main >= 2 * U
    first_head = lax.broadcasted_iota(jnp.int32, (T, LANES), 1) < F_DH

    def block(ref, blk):
        return ref[0, pl.ds(pl.multiple_of(blk * T, T), T), :]

    lane = lax.broadcasted_iota(jnp.int32, (T, LANES), 1)
    aug_lo = pl.program_id(1) * (2 * AUG_LANES)
    aug_mask = [(lane - (aug_lo + hd * AUG_LANES)).astype(jnp.uint32) < AUG_LANES for hd in range(2)]

    def scores(i, jk, buf, diag):
        q = block(q_ref, i)
        zero = jnp.zeros_like(q)
        q_heads = (jnp.where(first_head, q, zero), jnp.where(first_head, zero, q))
        k = jnp.concatenate([block(k_ref, jk), block(kaug_ref, jk)], axis=1)
        qa = block(qaug_ref, i)
        for hd in range(2):
            qx = jnp.concatenate([q_heads[hd], jnp.where(aug_mask[hd], qa, zero)], axis=1)
            if diag:
                for r0, width in ((0, H), (H, T)):
                    rows = lax.broadcasted_iota(jnp.int32, (H, width), 0)
                    cols = lax.broadcasted_iota(jnp.int32, (H, width), 1)
                    s = jnp.where(cols <= rows + r0, _dot_nt(qx[r0:r0 + H], k[0:width]), NEG_BIG)
                    m_new = jnp.broadcast_to(jnp.max(s, axis=-1, keepdims=True), (H, LANES))
                    m_all[i, hd, r0:r0 + H] = m_new
                    s_ref[buf, hd, r0:r0 + H, 0:width] = s
            else:
                s = _dot_nt(qx, k)
                m_old = m_all[i, hd]
                m_new = jnp.maximum(m_old, jnp.max(s, axis=-1, keepdims=True))
                al_ref[buf, hd] = jnp.exp2(m_old - m_new)
                m_all[i, hd] = m_new
                s_ref[buf, hd] = s

    def accumulate(i, jk, buf, diag):
        v = block(v_ref, jk)
        one = jnp.ones_like(v)
        v_heads = (jnp.where(first_head, v, one), jnp.where(first_head, one, v))
        for hd in range(2):
            if diag:
                for r0, width in ((0, H), (H, T)):
                    m = m_all[i, hd, r0:r0 + H]
                    p = jnp.exp2(s_ref[buf, hd, r0:r0 + H, 0:width] - jnp.tile(m, (1, width // LANES)))
                    accl_all[i, hd, r0:r0 + H] = _dot(p.astype(BF16), v_heads[hd][0:width])
            else:
                p = jnp.exp2(s_ref[buf, hd] - jnp.tile(m_all[i, hd], (1, T // LANES)))
                accl_all[i, hd] = al_ref[buf, hd] * accl_all[i, hd] + _dot(p.astype(BF16), v_heads[hd])

    def advance(i, jk):
        wrap = jk + 1 >= i
        return jnp.where(wrap, i + 1, i), jnp.where(wrap, 0, jk + 1)

    scores(0, 0, 0, True)

    def diag_body(jj, carry):
        t0 = U * jj
        for u in range(U):
            accumulate(t0 + u, t0 + u, u % 2, True)
            scores(t0 + u + 1, t0 + u + 1, (u + 1) % 2, True)
        return carry

    lax.fori_loop(0, nq // U - 1, diag_body, 0)
    for u in range(U):
        t = nq - U + u
        accumulate(t, t, u % 2, True)
        if u < U - 1:
            scores(t + 1, t + 1, (u + 1) % 2, True)
        else:
            scores(1, 0, 0, False)

    def main_body(_, cur):
        for u in range(U):
            nxt = advance(*cur)
            accumulate(*cur, u % 2, False)
            scores(*nxt, (u + 1) % 2, False)
            cur = nxt
        return cur

    cur = lax.fori_loop(0, n_main // U - 1, main_body, (jnp.int32(1), jnp.int32(0)))
    for u in range(U):
        nxt = advance(*cur)
        accumulate(*cur, u % 2, False)
        if u < U - 1:
            scores(*nxt, (u + 1) % 2, False)
        cur = nxt

    def finish(i, carry):
        outs = []
        for hd in range(2):
            a = accl_all[i, hd]
            outs.append(a * pltpu.roll(1.0 / a, F_DH, axis=1))
        out = jnp.where(first_head, outs[0], outs[1])
        rows = pl.ds(pl.multiple_of(i * T, T), T)
        y_ref[0, rows, :] = (out * _silu(z_ref[0, rows, :].astype(F32))).astype(BF16)
        return carry

    lax.fori_loop(0, nq, finish, 0)


def _fox(u, qaug, kaug):
    b, s, _ = u.shape
    T = FOX_TQ
    nq = s // T
    blocks_per_group = GROUP_W // LANES
    qb, kb, vb, zb = (slot * blocks_per_group for slot in (U_FQ, U_FK, U_FV, U_FZ))
    seq = lambda first: pl.BlockSpec((1, s, LANES), lambda bi, p: (bi, 0, first + p))
    return pl.pallas_call(
        _fox_kernel,
        grid=(b, F_HEADS // 2),
        in_specs=[
            seq(qb), seq(kb), seq(vb), seq(zb),
            pl.BlockSpec((1, s, LANES), lambda bi, p: (bi, 0, 0)),
            pl.BlockSpec((1, s, LANES), lambda bi, p: (bi, 0, 0)),
        ],
        out_specs=seq(0),
        out_shape=jax.ShapeDtypeStruct((b, s, F_W), BF16),
        scratch_shapes=[
            pltpu.VMEM((2, 2, T, T), F32),
            pltpu.VMEM((2, 2, T, LANES), F32),
            pltpu.VMEM((nq, 2, T, LANES), F32),
            pltpu.VMEM((nq, 2, T, LANES), F32),
        ],
        compiler_params=pltpu.CompilerParams(
            dimension_semantics=("arbitrary", "arbitrary"), vmem_limit_bytes=VMEM_LIMIT),
        name="fox",
    )(u, u, u, u, qaug, kaug)


def _bias_column_constants():
    sel = np.zeros((3, 3 * LANES, LANES), np.float32)
    aug1 = np.zeros((2, LANES), np.float32)
    for h in range(F_HEADS):
        src_lane = 2 * M_HEADS + h
        base = h * AUG_LANES
        for c in range(3):
            sel[0, c * LANES + src_lane, base + c] = 1.0
            sel[1, c * LANES + src_lane, base + 6 + c] = -1.0
            sel[2, c * LANES + src_lane, base + 3 + c] = 1.0
            aug1[0, base + 3 + c] = 1.0
            aug1[1, base + c] = 1.0
            aug1[1, base + 6 + c] = 1.0
    return jnp.asarray(sel, BF16), jnp.asarray(aug1, F32)


def _layer(x, c_pad, norm_g, w_ada, b_ada, w_in, conv_w, conv_b, b_igate, b_fgate_m,
           mlstm_norm_g, b_fgate_f, fox_qnorm_g, fox_knorm_g, w_out):
    b, s, d = x.shape
    mod = _adaln(c_pad, w_ada, b_ada[None, :])
    mod3 = mod[:b].reshape(b, 3, d)

    n_m = 5 * M_W
    n_f = 4 * F_W
    w_m = w_in[:, :n_m].astype(BF16)
    w_f = w_in[:, n_m + 2 * M_HEADS:n_m + 2 * M_HEADS + n_f].astype(BF16)
    w_gate = jnp.concatenate(
        [w_in[:, n_m:n_m + 2 * M_HEADS], w_in[:, n_m + 2 * M_HEADS + n_f:]], axis=1)
    n_gate = w_gate.shape[1]
    w_gate = jnp.pad(w_gate, ((0, 0), (0, LANES - n_gate))).astype(BF16)
    gate_b = jnp.pad(jnp.concatenate([b_igate, b_fgate_m, b_fgate_f]), (0, LANES - n_gate))[None, :]
    qg = jnp.tile(fox_qnorm_g, F_HEADS)[None, :] * (LOG2E / math.sqrt(F_DH))
    kg = jnp.tile(fox_knorm_g, F_HEADS)[None, :]
    head_id = jnp.arange(F_W // 2) // F_DH
    hsum = (head_id[:, None] == head_id[None, :]).astype(BF16)

    assert PROJ_ROWS == FOX_TQ
    sel, aug1 = _bias_column_constants()
    u, kt, gc, gtm, qaug, kaug = _proj(x, mod3, norm_g[None, :], w_m, w_f, w_gate, conv_w, conv_b[None, :],
                                       gate_b, qg, kg, hsum, sel, aug1)
    yf = _fox(u, qaug, kaug)
    return _mlstm_out(u, kt, gc, gtm, mlstm_norm_g[None, :], x, mod3, yf, w_out.astype(BF16))


def kernel(x, c, norm_g, w_ada, b_ada, w_in, conv_w, conv_b, b_igate, b_fgate_m, mlstm_norm_g,
           b_fgate_f, fox_qnorm_g, fox_knorm_g, w_out):
    depth = norm_g.shape[0]
    b = x.shape[0]
    c_pad = jnp.pad(c, ((0, (-b) % SUBLANES), (0, 0)))
    params = (norm_g, w_ada, b_ada, w_in, conv_w, conv_b, b_igate, b_fgate_m, mlstm_norm_g, b_fgate_f,
              fox_qnorm_g, fox_knorm_g, w_out)
    for l in range(depth):
        layer = [p.reshape(p.shape[1:]) if depth == 1 else p[l] for p in params]
        x = _layer(x, c_pad, *layer)
    return x
```

```python
import math

import jax
import jax.numpy as jnp
import numpy as np
from jax import lax
from jax.experimental import pallas as pl
from jax.experimental.pallas import tpu as pltpu

F32 = jnp.float32
BF16 = jnp.bfloat16

EPS = 1e-6
M_HEADS = 4
M_DH = 128
F_HEADS = 8
F_DH = 64
CONV_K = 4
M_W = M_HEADS * M_DH
F_W = F_HEADS * F_DH
GROUP_W = 512
W_MQ, W_MK, W_MV, W_MO, W_MZ, W_FQ, W_FK, W_FV, W_FZ = range(9)
U_MQ, U_MV, U_MO, U_MZ, U_FQ, U_FK, U_FV, U_FZ = range(8)
N_SLOTS = 8
LANES = 128
SUBLANES = 8
LOG2E = 1.4426950408889634
NEG_BIG = -1e30

PROJ_ROWS = 512
M_CHUNK = 256
M_STEP_ROWS = 2 * M_CHUNK
FOX_TQ = 512
FOX_UNROLL = 8
AUG_LANES = LANES // F_HEADS
VMEM_LIMIT = 56 * 1024 * 1024


def _dot(a, b):
    return jnp.dot(a, b, preferred_element_type=F32)


def _dot_nt(a, b):
    return lax.dot_general(a, b, (((1,), (1,)), ((), ())), preferred_element_type=F32)


def _silu(x):
    return x / (1.0 + jnp.exp2(x * (-LOG2E)))


def _log_sigmoid(x):
    return jnp.minimum(x, 0.0) - jnp.log1p(jnp.exp(-jnp.abs(x)))


def _adaln_kernel(c_ref, w_ref, b_ref, o_ref):
    c = c_ref[...]
    w = w_ref[...]
    c_hi = c.astype(BF16)
    c_lo = (c - c_hi.astype(F32)).astype(BF16)
    w_hi = w.astype(BF16)
    w_lo = (w - w_hi.astype(F32)).astype(BF16)
    acc = _dot(c_hi, w_hi) + _dot(c_hi, w_lo) + _dot(c_lo, w_hi)
    o_ref[...] = acc + b_ref[...]


def _adaln(c_pad, w_ada, b_ada):
    rows, d = c_pad.shape
    n = w_ada.shape[1]
    tn = d
    return pl.pallas_call(
        _adaln_kernel,
        grid=(n // tn,),
        in_specs=[
            pl.BlockSpec((rows, d), lambda j: (0, 0)),
            pl.BlockSpec((d, tn), lambda j: (0, j)),
            pl.BlockSpec((1, tn), lambda j: (0, j)),
        ],
        out_specs=pl.BlockSpec((rows, tn), lambda j: (0, j)),
        out_shape=jax.ShapeDtypeStruct((rows, n), F32),
        compiler_params=pltpu.CompilerParams(
            dimension_semantics=("arbitrary",), vmem_limit_bytes=VMEM_LIMIT),
        name="adaln",
    )(c_pad, w_ada, b_ada)


def _seg_cumsum(x, seg_row, n):
    shift = 1
    while shift < n:
        rolled = pltpu.roll(x, shift, axis=0)
        x = x + jnp.where(seg_row >= shift, rolled, 0.0)
        shift *= 2
    return x


def _proj_kernel(x_ref, mod_ref, ng_ref, wm_ref, wf_ref, wg_ref, cw_ref, cb_ref, gb_ref, qg_ref, kg_ref,
                 hsum_ref, sel_ref, aug1_ref, u_ref, kt_ref, gc_ref, gtm_ref, qaug_ref, kaug_ref, ext_ref, cum_ref):
    j = pl.program_id(1)
    ts = x_ref.shape[1]

    @pl.when(j == 0)
    def _():
        ext_ref[0:SUBLANES, :] = jnp.zeros((SUBLANES, ext_ref.shape[1]), F32)
        cum_ref[...] = jnp.zeros_like(cum_ref)

    x = x_ref[0]
    ms = jnp.mean(x * x, axis=-1, keepdims=True)
    shift = mod_ref[0, 0:1, :]
    gain = ng_ref[...] * (1.0 + mod_ref[0, 1:2, :])
    h = (x * lax.rsqrt(ms + EPS) * gain + shift).astype(BF16)

    def group(g):
        w_ref, first = (wm_ref, W_MQ) if g < W_FQ else (wf_ref, W_FQ)
        return _dot(h, w_ref[:, (g - first) * GROUP_W:(g - first + 1) * GROUP_W])

    def store(slot, val):
        u_ref[0, :, slot * GROUP_W:(slot + 1) * GROUP_W] = val.astype(BF16)

    ug = _dot(h, wg_ref[...]) + gb_ref[...]
    lane = lax.broadcasted_iota(jnp.int32, ug.shape, 1)
    row = lax.broadcasted_iota(jnp.int32, ug.shape, 0)
    lf = _log_sigmoid(ug) * LOG2E
    seg_row = jnp.where(lane < 2 * M_HEADS, row & (M_CHUNK - 1), row)
    cs = _seg_cumsum(lf, seg_row, ts)
    cs = cs + jnp.where(lane >= 2 * M_HEADS, cum_ref[0:1, :], 0.0)
    cum_ref[0:1, :] = cs[ts - 1:ts, :]
    gc = jnp.where(lane < M_HEADS, ug * LOG2E, cs)
    gc_ref[0] = gc
    gtm_ref[0] = gc.T[0:2 * M_HEADS, :]

    for g, slot, gain_ref in ((W_FQ, U_FQ, qg_ref), (W_FK, U_FK, kg_ref)):
        u = group(g)
        sq = (u * u).astype(BF16)
        half = hsum_ref.shape[0]
        ssq = jnp.concatenate([_dot(sq[:, 0:half], hsum_ref[...]), _dot(sq[:, half:], hsum_ref[...])], axis=1)
        store(slot, u * lax.rsqrt(ssq * (1.0 / F_DH) + EPS) * gain_ref[...])

    ext_ref[SUBLANES:, 0:GROUP_W] = group(W_MQ)
    ext_ref[SUBLANES:, GROUP_W:] = group(W_MK)
    conv = cb_ref[...] + cw_ref[0:1, :] * ext_ref[SUBLANES - 3:SUBLANES - 3 + ts, :]
    for t in range(1, CONV_K):
        lo = SUBLANES - 3 + t
        conv = conv + cw_ref[t:t + 1, :] * ext_ref[lo:lo + ts, :]
    ext_ref[0:SUBLANES, :] = ext_ref[ts:ts + SUBLANES, :]
    qk = _silu(conv)
    store(U_MQ, qk[:, 0:GROUP_W])
    kt_ref[0] = (qk[:, GROUP_W:] * (1.0 / math.sqrt(M_DH))).T.astype(BF16)

    for g, slot in ((W_MV, U_MV), (W_MO, U_MO), (W_MZ, U_MZ), (W_FV, U_FV), (W_FZ, U_FZ)):
        store(slot, group(g))

    def split3(a):
        hi = a.astype(BF16)
        r1 = a - hi.astype(F32)
        mid = r1.astype(BF16)
        lo = (r1 - mid.astype(F32)).astype(BF16)
        return jnp.concatenate([hi, mid, lo], axis=1)

    f_blk = jnp.broadcast_to(cs[0:1, :], (SUBLANES, LANES))
    blk_parts = split3(f_blk)
    k_const = _dot(blk_parts, sel_ref[1])[0:1, :] + aug1_ref[0:1, :]
    q_const = _dot(blk_parts, sel_ref[2])[0:1, :] + aug1_ref[1:2, :]
    kaug_ref[0] = (_dot(split3(cs[0:1, :] - cs), sel_ref[0]) + k_const).astype(BF16)
    qaug_ref[0] = jnp.broadcast_to(q_const, (ts, LANES)).astype(BF16)


def _proj(x, mod3, norm_g, w_m, w_f, w_gate, conv_w, conv_b, gate_b, qg, kg, hsum, sel, aug1):
    b, s, d = x.shape
    ts = PROJ_ROWS
    const = lambda *shape: pl.BlockSpec(shape, lambda bi, j: (0,) * len(shape))
    return pl.pallas_call(
        _proj_kernel,
        grid=(b, s // ts),
        in_specs=[
            pl.BlockSpec((1, ts, d), lambda bi, j: (bi, j, 0)),
            pl.BlockSpec((1, 3, d), lambda bi, j: (bi, 0, 0)),
            const(1, d),
            const(d, w_m.shape[1]),
            const(d, w_f.shape[1]),
            const(d, LANES),
            const(CONV_K, 2 * M_W),
            const(1, 2 * M_W),
            const(1, LANES),
            const(1, F_W),
            const(1, F_W),
            const(F_W // 2, F_W // 2),
            const(3, 3 * LANES, LANES),
            const(2, LANES),
        ],
        out_specs=[
            pl.BlockSpec((1, ts, N_SLOTS * GROUP_W), lambda bi, j: (bi, j, 0)),
            pl.BlockSpec((1, M_W, ts), lambda bi, j: (bi, 0, j)),
            pl.BlockSpec((1, ts, LANES), lambda bi, j: (bi, j, 0)),
            pl.BlockSpec((1, 2 * M_HEADS, ts), lambda bi, j: (bi, 0, j)),
            pl.BlockSpec((1, ts, LANES), lambda bi, j: (bi, j, 0)),
            pl.BlockSpec((1, ts, LANES), lambda bi, j: (bi, j, 0)),
        ],
        out_shape=[
            jax.ShapeDtypeStruct((b, s, N_SLOTS * GROUP_W), BF16),
            jax.ShapeDtypeStruct((b, M_W, s), BF16),
            jax.ShapeDtypeStruct((b, s, LANES), F32),
            jax.ShapeDtypeStruct((b, 2 * M_HEADS, s), F32),
            jax.ShapeDtypeStruct((b, s, LANES), BF16),
            jax.ShapeDtypeStruct((b, s, LANES), BF16),
        ],
        scratch_shapes=[
            pltpu.VMEM((ts + SUBLANES, 2 * M_W), F32),
            pltpu.VMEM((SUBLANES, LANES), F32),
        ],
        compiler_params=pltpu.CompilerParams(
            dimension_semantics=("arbitrary", "arbitrary"), vmem_limit_bytes=VMEM_LIMIT),
        name="proj",
    )(x, mod3, norm_g, w_m, w_f, w_gate, conv_w, conv_b, gate_b, qg, kg, hsum, sel, aug1)


def _mlstm_out_kernel(q_ref, kt_ref, v_ref, o_ref, z_ref, gc_ref, gt_ref, lng_ref, x_ref, mod_ref, yf_ref, w_ref,
                      out_ref, cn_ref, m_ref, ym_ref):
    c = pl.program_id(1)
    L = M_CHUNK

    @pl.when(c == 0)
    def _():
        cn_ref[...] = jnp.zeros_like(cn_ref)
        m_ref[...] = jnp.zeros_like(m_ref)

    rows = lax.broadcasted_iota(jnp.int32, (L, L), 0)
    cols = lax.broadcasted_iota(jnp.int32, (L, L), 1)
    causal = cols <= rows
    ones_blk = jnp.ones((L, M_DH), BF16)

    for ck, hd in [(ck, hd) for ck in range(q_ref.shape[1] // L) for hd in range(M_HEADS)]:
        t = slice(ck * L, (ck + 1) * L)
        sl = slice(hd * M_DH, (hd + 1) * M_DH)
        q = q_ref[0, t, sl]
        kt = kt_ref[0, sl, t]
        v1 = jnp.concatenate([v_ref[0, t, sl], ones_blk], axis=1)
        b_c = jnp.broadcast_to(gc_ref[0, t, M_HEADS + hd:M_HEADS + hd + 1], (L, LANES))
        r_row = gt_ref[0, hd:hd + 1, t] - gt_ref[0, M_HEADS + hd:M_HEADS + hd + 1, t]
        m_prev = m_ref[hd:hd + 1, :]
        cn = cn_ref[hd]

        rmat = jnp.where(causal, jnp.broadcast_to(r_row, (L, L)), NEG_BIG)
        g = jnp.maximum(m_prev, jnp.max(rmat, axis=-1, keepdims=True))
        sm = _dot(q, kt) * jnp.exp2(rmat - jnp.tile(g, (1, L // LANES)))
        w_inter = jnp.exp2(m_prev - g)
        numden = jnp.tile(w_inter, (1, 2)) * _dot(q, cn.astype(BF16)) + _dot(sm.astype(BF16), v1)
        e_neg_m = jnp.exp2(-(b_c + g))
        den = jnp.maximum(jnp.abs(numden[:, M_DH:]), e_neg_m)

        g_last = g[L - 1:L, :]
        ws_row = jnp.exp2(r_row - jnp.tile(g_last, (1, L // LANES)))
        ktw = kt * ws_row.astype(BF16)
        decay = jnp.exp2(m_prev - g_last)
        cn_ref[hd] = jnp.tile(decay, (1, 2)) * cn + _dot(ktw, v1)
        m_ref[hd:hd + 1, :] = b_c[L - 1:L, :] + g_last

        hm = numden[:, 0:M_DH] / (den * (1.0 + jnp.exp2(o_ref[0, t, sl].astype(F32) * (-LOG2E))))
        mu = jnp.mean(hm, axis=-1, keepdims=True)
        dv = hm - mu
        var = jnp.mean(dv * dv, axis=-1, keepdims=True)
        y = dv * lax.rsqrt(var + EPS) * lng_ref[:, sl]
        ym_ref[t, sl] = (y * _silu(z_ref[0, t, sl].astype(F32))).astype(BF16)

    proj = _dot(ym_ref[...], w_ref[0:M_W, :]) + _dot(yf_ref[0], w_ref[M_W:, :])
    out_ref[0] = x_ref[0] + mod_ref[0, 2:3, :] * proj


def _mlstm_out(u, kt, gc, gtm, ln_g, x, mod3, yf, w_out):
    b, s, d = x.shape
    L = M_STEP_ROWS
    col = lambda g: pl.BlockSpec((1, L, GROUP_W), lambda bi, c, g=g: (bi, c, g))
    rows = lambda width: pl.BlockSpec((1, L, width), lambda bi, c: (bi, c, 0))
    return pl.pallas_call(
        _mlstm_out_kernel,
        grid=(b, s // L),
        in_specs=[
            col(U_MQ),
            pl.BlockSpec((1, M_W, L), lambda bi, c: (bi, 0, c)),
            col(U_MV), col(U_MO), col(U_MZ),
            rows(LANES),
            pl.BlockSpec((1, 2 * M_HEADS, L), lambda bi, c: (bi, 0, c)),
            pl.BlockSpec((1, M_W), lambda bi, c: (0, 0)),
            rows(d),
            pl.BlockSpec((1, 3, d), lambda bi, c: (bi, 0, 0)),
            rows(F_W),
            pl.BlockSpec((M_W + F_W, d), lambda bi, c: (0, 0)),
        ],
        out_specs=rows(d),
        out_shape=jax.ShapeDtypeStruct((b, s, d), x.dtype),
        scratch_shapes=[
            pltpu.VMEM((M_HEADS, M_DH, 2 * M_DH), F32),
            pltpu.VMEM((SUBLANES, LANES), F32),
            pltpu.VMEM((L, M_W), BF16),
        ],
        compiler_params=pltpu.CompilerParams(
            dimension_semantics=("arbitrary", "arbitrary"), vmem_limit_bytes=VMEM_LIMIT),
        name="mlstm_out",
    )(u, kt, u, u, u, gc, gtm, ln_g, x, mod3, yf, w_out)


def _fox_kernel(q_ref, k_ref, v_ref, z_ref, qaug_ref, kaug_ref, y_ref, s_ref, al_ref, m_all, accl_all):
    T = FOX_TQ
    H = T // 2
    nq = q_ref.shape[1] // T
    n_main = nq * (nq - 1) // 2
    U = FOX_UNROLL
    assert U % 2 == 0 and nq % U == 0 and n_main % U == 0 and n_main >= 2 * U
    first_head = lax.broadcasted_iota(jnp.int32, (T, LANES), 1) < F_DH

    def block(ref, blk):
        return ref[0, pl.ds(pl.multiple_of(blk * T, T), T), :]

    lane = lax.broadcasted_iota(jnp.int32, (T, LANES), 1)
    aug_lo = pl.program_id(1) * (2 * AUG_LANES)
    aug_mask = [(lane - (aug_lo + hd * AUG_LANES)).astype(jnp.uint32) < AUG_LANES for hd in range(2)]

    def scores(i, jk, buf, diag):
        q = block(q_ref, i)
        zero = jnp.zeros_like(q)
        q_heads = (jnp.where(first_head, q, zero), jnp.where(first_head, zero, q))
        k = jnp.concatenate([block(k_ref, jk), block(kaug_ref, jk)], axis=1)
        qa = block(qaug_ref, i)
        for hd in range(2):
            qx = jnp.concatenate([q_heads[hd], jnp.where(aug_mask[hd], qa, zero)], axis=1)
            if diag:
                for r0, width in ((0, H), (H, T)):
                    rows = lax.broadcasted_iota(jnp.int32, (H, width), 0)
                    cols = lax.broadcasted_iota(jnp.int32, (H, width), 1)
                    s = jnp.where(cols <= rows + r0, _dot_nt(qx[r0:r0 + H], k[0:width]), NEG_BIG)
                    m_new = jnp.broadcast_to(jnp.max(s, axis=-1, keepdims=True), (H, LANES))
                    m_all[i, hd, r0:r0 + H] = m_new
                    s_ref[buf, hd, r0:r0 + H, 0:width] = s
            else:
                s = _dot_nt(qx, k)
                m_old = m_all[i, hd]
                m_new = jnp.maximum(m_old, jnp.max(s, axis=-1, keepdims=True))
                al_ref[buf, hd] = jnp.exp2(m_old - m_new)
                m_all[i, hd] = m_new
                s_ref[buf, hd] = s

    def accumulate(i, jk, buf, diag):
        v = block(v_ref, jk)
        one = jnp.ones_like(v)
        v_heads = (jnp.where(first_head, v, one), jnp.where(first_head, one, v))
        for hd in range(2):
            if diag:
                for r0, width in ((0, H), (H, T)):
                    m = m_all[i, hd, r0:r0 + H]
                    p = jnp.exp2(s_ref[buf, hd, r0:r0 + H, 0:width] - jnp.tile(m, (1, width // LANES)))
                    accl_all[i, hd, r0:r0 + H] = _dot(p.astype(BF16), v_heads[hd][0:width])
            else:
                p = jnp.exp2(s_ref[buf, hd] - jnp.tile(m_all[i, hd], (1, T // LANES)))
                accl_all[i, hd] = al_ref[buf, hd] * accl_all[i, hd] + _dot(p.astype(BF16), v_heads[hd])

    def advance(i, jk):
        wrap = jk + 1 >= i
        return jnp.where(wrap, i + 1, i), jnp.where(wrap, 0, jk + 1)

    scores(0, 0, 0, True)

    def diag_body(jj, carry):
        t0 = U * jj
        for u in range(U):
            accumulate(t0 + u, t0 + u, u % 2, True)
            scores(t0 + u + 1, t0 + u + 1, (u + 1) % 2, True)
        return carry

    lax.fori_loop(0, nq // U - 1, diag_body, 0)
    for u in range(U):
        t = nq - U + u
        accumulate(t, t, u % 2, True)
        if u < U - 1:
            scores(t + 1, t + 1, (u + 1) % 2, True)
        else:
            scores(1, 0, 0, False)

    def main_body(_, cur):
        for u in range(U):
            nxt = advance(*cur)
            accumulate(*cur, u % 2, False)
            scores(*nxt, (u + 1) % 2, False)
            cur = nxt
        return cur

    cur = lax.fori_loop(0, n_main // U - 1, main_body, (jnp.int32(1), jnp.int32(0)))
    for u in range(U):
        nxt = advance(*cur)
        accumulate(*cur, u % 2, False)
        if u < U - 1:
            scores(*nxt, (u + 1) % 2, False)
        cur = nxt

    def finish(i, carry):
        outs = []
        for hd in range(2):
            a = accl_all[i, hd]
            outs.append(a * pltpu.roll(1.0 / a, F_DH, axis=1))
        out = jnp.where(first_head, outs[0], outs[1])
        rows = pl.ds(pl.multiple_of(i * T, T), T)
        y_ref[0, rows, :] = (out * _silu(z_ref[0, rows, :].astype(F32))).astype(BF16)
        return carry

    lax.fori_loop(0, nq, finish, 0)


def _fox(u, qaug, kaug):
    b, s, _ = u.shape
    T = FOX_TQ
    nq = s // T
    blocks_per_group = GROUP_W // LANES
    qb, kb, vb, zb = (slot * blocks_per_group for slot in (U_FQ, U_FK, U_FV, U_FZ))
    seq = lambda first: pl.BlockSpec((1, s, LANES), lambda bi, p: (bi, 0, first + p))
    return pl.pallas_call(
        _fox_kernel,
        grid=(b, F_HEADS // 2),
        in_specs=[
            seq(qb), seq(kb), seq(vb), seq(zb),
            pl.BlockSpec((1, s, LANES), lambda bi, p: (bi, 0, 0)),
            pl.BlockSpec((1, s, LANES), lambda bi, p: (bi, 0, 0)),
        ],
        out_specs=seq(0),
        out_shape=jax.ShapeDtypeStruct((b, s, F_W), BF16),
        scratch_shapes=[
            pltpu.VMEM((2, 2, T, T), F32),
            pltpu.VMEM((2, 2, T, LANES), F32),
            pltpu.VMEM((nq, 2, T, LANES), F32),
            pltpu.VMEM((nq, 2, T, LANES), F32),
        ],
        compiler_params=pltpu.CompilerParams(
            dimension_semantics=("arbitrary", "arbitrary"), vmem_limit_bytes=VMEM_LIMIT),
        name="fox",
    )(u, u, u, u, qaug, kaug)


def _bias_column_constants():
    sel = np.zeros((3, 3 * LANES, LANES), np.float32)
    aug1 = np.zeros((2, LANES), np.float32)
    for h in range(F_HEADS):
        src_lane = 2 * M_HEADS + h
        base = h * AUG_LANES
        for c in range(3):
            sel[0, c * LANES + src_lane, base + c] = 1.0
            sel[1, c * LANES + src_lane, base + 6 + c] = -1.0
            sel[2, c * LANES + src_lane, base + 3 + c] = 1.0
            aug1[0, base + 3 + c] = 1.0
            aug1[1, base + c] = 1.0
            aug1[1, base + 6 + c] = 1.0
    return jnp.asarray(sel, BF16), jnp.asarray(aug1, F32)


def _layer(x, c_pad, norm_g, w_ada, b_ada, w_in, conv_w, conv_b, b_igate, b_fgate_m,
           mlstm_norm_g, b_fgate_f, fox_qnorm_g, fox_knorm_g, w_out):
    b, s, d = x.shape
    mod = _adaln(c_pad, w_ada, b_ada[None, :])
    mod3 = mod[:b].reshape(b, 3, d)

    n_m = 5 * M_W
    n_f = 4 * F_W
    w_bf = w_in.astype(BF16)
    w_m = w_bf[:, :n_m]
    w_f = w_bf[:, n_m + 2 * M_HEADS:n_m + 2 * M_HEADS + n_f]
    w_gate = jnp.concatenate(
        [w_bf[:, n_m:n_m + 2 * M_HEADS], w_bf[:, n_m + 2 * M_HEADS + n_f:]], axis=1)
    n_gate = w_gate.shape[1]
    w_gate = jnp.pad(w_gate, ((0, 0), (0, LANES - n_gate)))
    gate_b = jnp.pad(jnp.concatenate([b_igate, b_fgate_m, b_fgate_f]), (0, LANES - n_gate))[None, :]
    qg = jnp.tile(fox_qnorm_g, F_HEADS)[None, :] * (LOG2E / math.sqrt(F_DH))
    kg = jnp.tile(fox_knorm_g, F_HEADS)[None, :]
    head_id = jnp.arange(F_W // 2) // F_DH
    hsum = (head_id[:, None] == head_id[None, :]).astype(BF16)

    assert PROJ_ROWS == FOX_TQ
    sel, aug1 = _bias_column_constants()
    u, kt, gc, gtm, qaug, kaug = _proj(x, mod3, norm_g[None, :], w_m, w_f, w_gate, conv_w, conv_b[None, :],
                                       gate_b, qg, kg, hsum, sel, aug1)
    yf = _fox(u, qaug, kaug)
    return _mlstm_out(u, kt, gc, gtm, mlstm_norm_g[None, :], x, mod3, yf, w_out.astype(BF16))


def kernel(x, c, norm_g, w_ada, b_ada, w_in, conv_w, conv_b, b_igate, b_fgate_m, mlstm_norm_g,
           b_fgate_f, fox_qnorm_g, fox_knorm_g, w_out):
    depth = norm_g.shape[0]
    b = x.shape[0]
    c_pad = jnp.pad(c, ((0, (-b) % SUBLANES), (0, 0)))
    params = (norm_g, w_ada, b_ada, w_in, conv_w, conv_b, b_igate, b_fgate_m, mlstm_norm_g, b_fgate_f,
              fox_qnorm_g, fox_knorm_g, w_out)
    for l in range(depth):
        layer = [p.reshape(p.shape[1:]) if depth == 1 else p[l] for p in params]
        x = _layer(x, c_pad, *layer)
    return x
```

```python
import math

import jax
import jax.numpy as jnp
import numpy as np
from jax import lax
from jax.experimental import pallas as pl
from jax.experimental.pallas import tpu as pltpu

F32 = jnp.float32
BF16 = jnp.bfloat16

EPS = 1e-6
M_HEADS = 4
M_DH = 128
F_HEADS = 8
F_DH = 64
CONV_K = 4
M_W = M_HEADS * M_DH
F_W = F_HEADS * F_DH
GROUP_W = 512
W_MQ, W_MK, W_MV, W_MO, W_MZ, W_FQ, W_FK, W_FV, W_FZ = range(9)
U_MQ, U_MV, U_MO, U_MZ, U_FQ, U_FK, U_FV, U_FZ = range(8)
N_SLOTS = 8
LANES = 128
SUBLANES = 8
LOG2E = 1.4426950408889634
NEG_BIG = -1e30

PROJ_ROWS = 512
M_CHUNK = 256
M_STEP_ROWS = 2 * M_CHUNK
FOX_TQ = 512
FOX_UNROLL = 8
AUG_LANES = LANES // F_HEADS
VMEM_LIMIT = 56 * 1024 * 1024


def _dot(a, b):
    return jnp.dot(a, b, preferred_element_type=F32)


def _dot_nt(a, b):
    return lax.dot_general(a, b, (((1,), (1,)), ((), ())), preferred_element_type=F32)


def _silu(x):
    return x / (1.0 + jnp.exp2(x * (-LOG2E)))


def _log_sigmoid(x):
    return jnp.minimum(x, 0.0) - jnp.log1p(jnp.exp(-jnp.abs(x)))


def _adaln_kernel(c_ref, w_ref, b_ref, o_ref):
    c = c_ref[...]
    w = w_ref[...]
    c_hi = c.astype(BF16)
    c_lo = (c - c_hi.astype(F32)).astype(BF16)
    w_hi = w.astype(BF16)
    w_lo = (w - w_hi.astype(F32)).astype(BF16)
    acc = _dot(c_hi, w_hi) + _dot(c_hi, w_lo) + _dot(c_lo, w_hi)
    o_ref[...] = acc + b_ref[...]


def _adaln(c_pad, w_ada, b_ada):
    rows, d = c_pad.shape
    n = w_ada.shape[1]
    tn = d
    return pl.pallas_call(
        _adaln_kernel,
        grid=(n // tn,),
        in_specs=[
            pl.BlockSpec((rows, d), lambda j: (0, 0)),
            pl.BlockSpec((d, tn), lambda j: (0, j)),
            pl.BlockSpec((1, tn), lambda j: (0, j)),
        ],
        out_specs=pl.BlockSpec((rows, tn), lambda j: (0, j)),
        out_shape=jax.ShapeDtypeStruct((rows, n), F32),
        compiler_params=pltpu.CompilerParams(
            dimension_semantics=("arbitrary",), vmem_limit_bytes=VMEM_LIMIT),
        name="adaln",
    )(c_pad, w_ada, b_ada)


def _seg_cumsum(x, seg_row, n):
    shift = 1
    while shift < n:
        rolled = pltpu.roll(x, shift, axis=0)
        x = x + jnp.where(seg_row >= shift, rolled, 0.0)
        shift *= 2
    return x


def _proj_kernel(x_ref, mod_ref, ng_ref, wm_ref, wf_ref, wg_ref, cw_ref, cb_ref, gb_ref, qg_ref, kg_ref,
                 hsum_ref, sel_ref, aug1_ref, u_ref, kt_ref, gc_ref, gtm_ref, qaug_ref, kaug_ref, ext_ref, cum_ref):
    j = pl.program_id(1)
    ts = x_ref.shape[1]

    @pl.when(j == 0)
    def _():
        ext_ref[0:SUBLANES, :] = jnp.zeros((SUBLANES, ext_ref.shape[1]), F32)
        cum_ref[...] = jnp.zeros_like(cum_ref)

    x = x_ref[0]
    ms = jnp.mean(x * x, axis=-1, keepdims=True)
    shift = mod_ref[0, 0:1, :]
    gain = ng_ref[...] * (1.0 + mod_ref[0, 1:2, :])
    h = (x * lax.rsqrt(ms + EPS) * gain + shift).astype(BF16)

    def group(g):
        w_ref, first = (wm_ref, W_MQ) if g < W_FQ else (wf_ref, W_FQ)
        return _dot(h, w_ref[:, (g - first) * GROUP_W:(g - first + 1) * GROUP_W])

    def store(slot, val):
        u_ref[0, :, slot * GROUP_W:(slot + 1) * GROUP_W] = val.astype(BF16)

    ug = _dot(h, wg_ref[...]) + gb_ref[...]
    lane = lax.broadcasted_iota(jnp.int32, ug.shape, 1)
    row = lax.broadcasted_iota(jnp.int32, ug.shape, 0)
    lf = _log_sigmoid(ug) * LOG2E
    seg_row = jnp.where(lane < 2 * M_HEADS, row & (M_CHUNK - 1), row)
    cs = _seg_cumsum(lf, seg_row, ts)
    cs = cs + jnp.where(lane >= 2 * M_HEADS, cum_ref[0:1, :], 0.0)
    cum_ref[0:1, :] = cs[ts - 1:ts, :]
    gc = jnp.where(lane < M_HEADS, ug * LOG2E, cs)
    gc_ref[0] = gc
    gtm_ref[0] = gc.T[0:2 * M_HEADS, :]

    for g, slot, gain_ref in ((W_FQ, U_FQ, qg_ref), (W_FK, U_FK, kg_ref)):
        u = group(g)
        sq = (u * u).astype(BF16)
        half = hsum_ref.shape[0]
        ssq = jnp.concatenate([_dot(sq[:, 0:half], hsum_ref[...]), _dot(sq[:, half:], hsum_ref[...])], axis=1)
        store(slot, u * lax.rsqrt(ssq * (1.0 / F_DH) + EPS) * gain_ref[...])

    ext_ref[SUBLANES:, 0:GROUP_W] = group(W_MQ)
    ext_ref[SUBLANES:, GROUP_W:] = group(W_MK)
    conv = cb_ref[...] + cw_ref[0:1, :] * ext_ref[SUBLANES - 3:SUBLANES - 3 + ts, :]
    for t in range(1, CONV_K):
        lo = SUBLANES - 3 + t
        conv = conv + cw_ref[t:t + 1, :] * ext_ref[lo:lo + ts, :]
    ext_ref[0:SUBLANES, :] = ext_ref[ts:ts + SUBLANES, :]
    qk = _silu(conv)
    store(U_MQ, qk[:, 0:GROUP_W])
    kt_ref[0] = (qk[:, GROUP_W:] * (1.0 / math.sqrt(M_DH))).T.astype(BF16)

    for g, slot in ((W_MV, U_MV), (W_MO, U_MO), (W_MZ, U_MZ), (W_FV, U_FV), (W_FZ, U_FZ)):
        store(slot, group(g))

    def split3(a):
        hi = a.astype(BF16)
        r1 = a - hi.astype(F32)
        mid = r1.astype(BF16)
        lo = (r1 - mid.astype(F32)).astype(BF16)
        return jnp.concatenate([hi, mid, lo], axis=1)

    f_blk = jnp.broadcast_to(cs[0:1, :], (SUBLANES, LANES))
    blk_parts = split3(f_blk)
    k_const = _dot(blk_parts, sel_ref[1])[0:1, :] + aug1_ref[0:1, :]
    q_const = _dot(blk_parts, sel_ref[2])[0:1, :] + aug1_ref[1:2, :]
    kaug_ref[0] = (_dot(split3(cs[0:1, :] - cs), sel_ref[0]) + k_const).astype(BF16)
    qaug_ref[0] = jnp.broadcast_to(q_const, (ts, LANES)).astype(BF16)


def _proj(x, mod3, norm_g, w_m, w_f, w_gate, conv_w, conv_b, gate_b, qg, kg, hsum, sel, aug1):
    b, s, d = x.shape
    ts = PROJ_ROWS
    const = lambda *shape: pl.BlockSpec(shape, lambda bi, j: (0,) * len(shape))
    return pl.pallas_call(
        _proj_kernel,
        grid=(b, s // ts),
        in_specs=[
            pl.BlockSpec((1, ts, d), lambda bi, j: (bi, j, 0)),
            pl.BlockSpec((1, 3, d), lambda bi, j: (bi, 0, 0)),
            const(1, d),
            const(d, w_m.shape[1]),
            const(d, w_f.shape[1]),
            const(d, LANES),
            const(CONV_K, 2 * M_W),
            const(1, 2 * M_W),
            const(1, LANES),
            const(1, F_W),
            const(1, F_W),
            const(F_W // 2, F_W // 2),
            const(3, 3 * LANES, LANES),
            const(2, LANES),
        ],
        out_specs=[
            pl.BlockSpec((1, ts, N_SLOTS * GROUP_W), lambda bi, j: (bi, j, 0)),
            pl.BlockSpec((1, M_W, ts), lambda bi, j: (bi, 0, j)),
            pl.BlockSpec((1, ts, LANES), lambda bi, j: (bi, j, 0)),
            pl.BlockSpec((1, 2 * M_HEADS, ts), lambda bi, j: (bi, 0, j)),
            pl.BlockSpec((1, ts, LANES), lambda bi, j: (bi, j, 0)),
            pl.BlockSpec((1, ts, LANES), lambda bi, j: (bi, j, 0)),
        ],
        out_shape=[
            jax.ShapeDtypeStruct((b, s, N_SLOTS * GROUP_W), BF16),
            jax.ShapeDtypeStruct((b, M_W, s), BF16),
            jax.ShapeDtypeStruct((b, s, LANES), F32),
            jax.ShapeDtypeStruct((b, 2 * M_HEADS, s), F32),
            jax.ShapeDtypeStruct((b, s, LANES), BF16),
            jax.ShapeDtypeStruct((b, s, LANES), BF16),
        ],
        scratch_shapes=[
            pltpu.VMEM((ts + SUBLANES, 2 * M_W), F32),
            pltpu.VMEM((SUBLANES, LANES), F32),
        ],
        compiler_params=pltpu.CompilerParams(
            dimension_semantics=("arbitrary", "arbitrary"), vmem_limit_bytes=VMEM_LIMIT),
        name="proj",
    )(x, mod3, norm_g, w_m, w_f, w_gate, conv_w, conv_b, gate_b, qg, kg, hsum, sel, aug1)


def _mlstm_out_kernel(q_ref, kt_ref, v_ref, o_ref, z_ref, gc_ref, gt_ref, lng_ref, x_ref, mod_ref, yf_ref, w_ref,
                      out_ref, cn_ref, m_ref, ym_ref):
    c = pl.program_id(1)
    L = M_CHUNK

    @pl.when(c == 0)
    def _():
        cn_ref[...] = jnp.zeros_like(cn_ref)
        m_ref[...] = jnp.zeros_like(m_ref)

    rows = lax.broadcasted_iota(jnp.int32, (L, L), 0)
    cols = lax.broadcasted_iota(jnp.int32, (L, L), 1)
    causal = cols <= rows
    ones_blk = jnp.ones((L, M_DH), BF16)

    for ck, hd in [(ck, hd) for ck in range(q_ref.shape[1] // L) for hd in range(M_HEADS)]:
        t = slice(ck * L, (ck + 1) * L)
        sl = slice(hd * M_DH, (hd + 1) * M_DH)
        q = q_ref[0, t, sl]
        kt = kt_ref[0, sl, t]
        v1 = jnp.concatenate([v_ref[0, t, sl], ones_blk], axis=1)
        b_c = jnp.broadcast_to(gc_ref[0, t, M_HEADS + hd:M_HEADS + hd + 1], (L, LANES))
        r_row = gt_ref[0, hd:hd + 1, t] - gt_ref[0, M_HEADS + hd:M_HEADS + hd + 1, t]
        m_prev = m_ref[hd:hd + 1, :]
        cn = cn_ref[hd]

        rmat = jnp.where(causal, jnp.broadcast_to(r_row, (L, L)), NEG_BIG)
        g = jnp.maximum(m_prev, jnp.max(rmat, axis=-1, keepdims=True))
        sm = _dot(q, kt) * jnp.exp2(rmat - jnp.tile(g, (1, L // LANES)))
        w_inter = jnp.exp2(m_prev - g)
        numden = jnp.tile(w_inter, (1, 2)) * _dot(q, cn.astype(BF16)) + _dot(sm.astype(BF16), v1)
        e_neg_m = jnp.exp2(-(b_c + g))
        den = jnp.maximum(jnp.abs(numden[:, M_DH:]), e_neg_m)

        g_last = g[L - 1:L, :]
        ws_row = jnp.exp2(r_row - jnp.tile(g_last, (1, L // LANES)))
        ktw = kt * ws_row.astype(BF16)
        decay = jnp.exp2(m_prev - g_last)
        cn_ref[hd] = jnp.tile(decay, (1, 2)) * cn + _dot(ktw, v1)
        m_ref[hd:hd + 1, :] = b_c[L - 1:L, :] + g_last

        hm = numden[:, 0:M_DH] / (den * (1.0 + jnp.exp2(o_ref[0, t, sl].astype(F32) * (-LOG2E))))
        mu = jnp.mean(hm, axis=-1, keepdims=True)
        dv = hm - mu
        var = jnp.mean(dv * dv, axis=-1, keepdims=True)
        y = dv * lax.rsqrt(var + EPS) * lng_ref[:, sl]
        ym_ref[t, sl] = (y * _silu(z_ref[0, t, sl].astype(F32))).astype(BF16)

    proj = _dot(ym_ref[...], w_ref[0:M_W, :]) + _dot(yf_ref[0], w_ref[M_W:, :])
    out_ref[0] = x_ref[0] + mod_ref[0, 2:3, :] * proj


def _mlstm_out(u, kt, gc, gtm, ln_g, x, mod3, yf, w_out):
    b, s, d = x.shape
    L = M_STEP_ROWS
    col = lambda g: pl.BlockSpec((1, L, GROUP_W), lambda bi, c, g=g: (bi, c, g))
    rows = lambda width: pl.BlockSpec((1, L, width), lambda bi, c: (bi, c, 0))
    return pl.pallas_call(
        _mlstm_out_kernel,
        grid=(b, s // L),
        in_specs=[
            col(U_MQ),
            pl.BlockSpec((1, M_W, L), lambda bi, c: (bi, 0, c)),
            col(U_MV), col(U_MO), col(U_MZ),
            rows(LANES),
            pl.BlockSpec((1, 2 * M_HEADS, L), lambda bi, c: (bi, 0, c)),
            pl.BlockSpec((1, M_W), lambda bi, c: (0, 0)),
            rows(d),
            pl.BlockSpec((1, 3, d), lambda bi, c: (bi, 0, 0)),
            rows(F_W),
            pl.BlockSpec((M_W + F_W, d), lambda bi, c: (0, 0)),
        ],
        out_specs=rows(d),
        out_shape=jax.ShapeDtypeStruct((b, s, d), x.dtype),
        scratch_shapes=[
            pltpu.VMEM((M_HEADS, M_DH, 2 * M_DH), F32),
            pltpu.VMEM((SUBLANES, LANES), F32),
            pltpu.VMEM((L, M_W), BF16),
        ],
        compiler_params=pltpu.CompilerParams(
            dimension_semantics=("arbitrary", "arbitrary"), vmem_limit_bytes=VMEM_LIMIT),
        name="mlstm_out",
    )(u, kt, u, u, u, gc, gtm, ln_g, x, mod3, yf, w_out)


def _fox_kernel(q_ref, k_ref, v_ref, z_ref, qaug_ref, kaug_ref, y_ref, s_ref, al_ref, m_all, accl_all):
    T = FOX_TQ
    H = T // 2
    nq = q_ref.shape[1] // T
    n_main = nq * (nq - 1) // 2
    U = FOX_UNROLL
    assert U % 2 == 0 and nq % U == 0 and n_main % U == 0 and n_main >= 2 * U
    first_head = lax.broadcasted_iota(jnp.int32, (T, LANES), 1) < F_DH

    def block(ref, blk):
        return ref[0, pl.ds(pl.multiple_of(blk * T, T), T), :]

    lane = lax.broadcasted_iota(jnp.int32, (T, LANES), 1)
    aug_lo = pl.program_id(1) * (2 * AUG_LANES)
    aug_mask = [(lane - (aug_lo + hd * AUG_LANES)).astype(jnp.uint32) < AUG_LANES for hd in range(2)]

    def scores(i, jk, buf, diag):
        q = block(q_ref, i)
        zero = jnp.zeros_like(q)
        q_heads = (jnp.where(first_head, q, zero), jnp.where(first_head, zero, q))
        k = jnp.concatenate([block(k_ref, jk), block(kaug_ref, jk)], axis=1)
        qa = block(qaug_ref, i)
        for hd in range(2):
            qx = jnp.concatenate([q_heads[hd], jnp.where(aug_mask[hd], qa, zero)], axis=1)
            if diag:
                for r0, width in ((0, H), (H, T)):
                    rows = lax.broadcasted_iota(jnp.int32, (H, width), 0)
                    cols = lax.broadcasted_iota(jnp.int32, (H, width), 1)
                    s = jnp.where(cols <= rows + r0, _dot_nt(qx[r0:r0 + H], k[0:width]), NEG_BIG)
                    m_new = jnp.broadcast_to(jnp.max(s, axis=-1, keepdims=True), (H, LANES))
                    m_all[i, hd, r0:r0 + H] = m_new
                    s_ref[buf, hd, r0:r0 + H, 0:width] = s
            else:
                s = _dot_nt(qx, k)
                m_old = m_all[i, hd]
                m_new = jnp.maximum(m_old, jnp.max(s, axis=-1, keepdims=True))
                al_ref[buf, hd] = jnp.exp2(m_old - m_new)
                m_all[i, hd] = m_new
                s_ref[buf, hd] = s

    def accumulate(i, jk, buf, diag):
        v = block(v_ref, jk)
        one = jnp.ones_like(v)
        v_heads = (jnp.where(first_head, v, one), jnp.where(first_head, one, v))
        for hd in range(2):
            if diag:
                for r0, width in ((0, H), (H, T)):
                    m = m_all[i, hd, r0:r0 + H]
                    p = jnp.exp2(s_ref[buf, hd, r0:r0 + H, 0:width] - jnp.tile(m, (1, width // LANES)))
                    accl_all[i, hd, r0:r0 + H] = _dot(p.astype(BF16), v_heads[hd][0:width])
            else:
                p = jnp.exp2(s_ref[buf, hd] - jnp.tile(m_all[i, hd], (1, T // LANES)))
                accl_all[i, hd] = al_ref[buf, hd] * accl_all[i, hd] + _dot(p.astype(BF16), v_heads[hd])

    def advance(i, jk):
        wrap = jk + 1 >= i
        return jnp.where(wrap, i + 1, i), jnp.where(wrap, 0, jk + 1)

    scores(0, 0, 0, True)

    def diag_body(jj, carry):
        t0 = U * jj
        for u in range(U):
            accumulate(t0 + u, t0 + u, u % 2, True)
            scores(t0 + u + 1, t0 + u + 1, (u + 1) % 2, True)
        return carry

    lax.fori_loop(0, nq // U - 1, diag_body, 0)
    for u in range(U):
        t = nq - U + u
        accumulate(t, t, u % 2, True)
        if u < U - 1:
            scores(t + 1, t + 1, (u + 1) % 2, True)
        else:
            scores(1, 0, 0, False)

    def main_body(_, cur):
        for u in range(U):
            nxt = advance(*cur)
            accumulate(*cur, u % 2, False)
            scores(*nxt, (u + 1) % 2, False)
            cur = nxt
        return cur

    cur = lax.fori_loop(0, n_main // U - 1, main_body, (jnp.int32(1), jnp.int32(0)))
    for u in range(U):
        nxt = advance(*cur)
        accumulate(*cur, u % 2, False)
        if u < U - 1:
            scores(*nxt, (u + 1) % 2, False)
        cur = nxt

    def finish(i, carry):
        a0 = accl_all[i, 0]
        a1 = accl_all[i, 1]
        inv = pltpu.roll(1.0 / jnp.where(first_head, a1, a0), F_DH, axis=1)
        out = jnp.where(first_head, a0, a1) * inv
        rows = pl.ds(pl.multiple_of(i * T, T), T)
        y_ref[0, rows, :] = (out * _silu(z_ref[0, rows, :].astype(F32))).astype(BF16)
        return carry

    lax.fori_loop(0, nq, finish, 0)


def _fox(u, qaug, kaug):
    b, s, _ = u.shape
    T = FOX_TQ
    nq = s // T
    blocks_per_group = GROUP_W // LANES
    qb, kb, vb, zb = (slot * blocks_per_group for slot in (U_FQ, U_FK, U_FV, U_FZ))
    seq = lambda first: pl.BlockSpec((1, s, LANES), lambda bi, p: (bi, 0, first + p))
    return pl.pallas_call(
        _fox_kernel,
        grid=(b, F_HEADS // 2),
        in_specs=[
            seq(qb), seq(kb), seq(vb), seq(zb),
            pl.BlockSpec((1, s, LANES), lambda bi, p: (bi, 0, 0)),
            pl.BlockSpec((1, s, LANES), lambda bi, p: (bi, 0, 0)),
        ],
        out_specs=seq(0),
        out_shape=jax.ShapeDtypeStruct((b, s, F_W), BF16),
        scratch_shapes=[
            pltpu.VMEM((2, 2, T, T), F32),
            pltpu.VMEM((2, 2, T, LANES), F32),
            pltpu.VMEM((nq, 2, T, LANES), F32),
            pltpu.VMEM((nq, 2, T, LANES), F32),
        ],
        compiler_params=pltpu.CompilerParams(
            dimension_semantics=("arbitrary", "arbitrary"), vmem_limit_bytes=VMEM_LIMIT),
        name="fox",
    )(u, u, u, u, qaug, kaug)


def _bias_column_constants():
    sel = np.zeros((3, 3 * LANES, LANES), np.float32)
    aug1 = np.zeros((2, LANES), np.float32)
    for h in range(F_HEADS):
        src_lane = 2 * M_HEADS + h
        base = h * AUG_LANES
        for c in range(3):
            sel[0, c * LANES + src_lane, base + c] = 1.0
            sel[1, c * LANES + src_lane, base + 6 + c] = -1.0
            sel[2, c * LANES + src_lane, base + 3 + c] = 1.0
            aug1[0, base + 3 + c] = 1.0
            aug1[1, base + c] = 1.0
            aug1[1, base + 6 + c] = 1.0
    return jnp.asarray(sel, BF16), jnp.asarray(aug1, F32)


def _layer(x, c_pad, norm_g, w_ada, b_ada, w_in, conv_w, conv_b, b_igate, b_fgate_m,
           mlstm_norm_g, b_fgate_f, fox_qnorm_g, fox_knorm_g, w_out):
    b, s, d = x.shape
    mod = _adaln(c_pad, w_ada, b_ada[None, :])
    mod3 = mod[:b].reshape(b, 3, d)

    n_m = 5 * M_W
    n_f = 4 * F_W
    w_bf = w_in.astype(BF16)
    w_m = w_bf[:, :n_m]
    w_f = w_bf[:, n_m + 2 * M_HEADS:n_m + 2 * M_HEADS + n_f]
    w_gate = jnp.concatenate(
        [w_bf[:, n_m:n_m + 2 * M_HEADS], w_bf[:, n_m + 2 * M_HEADS + n_f:]], axis=1)
    n_gate = w_gate.shape[1]
    w_gate = jnp.pad(w_gate, ((0, 0), (0, LANES - n_gate)))
    gate_b = jnp.pad(jnp.concatenate([b_igate, b_fgate_m, b_fgate_f]), (0, LANES - n_gate))[None, :]
    qg = jnp.tile(fox_qnorm_g, F_HEADS)[None, :] * (LOG2E / math.sqrt(F_DH))
    kg = jnp.tile(fox_knorm_g, F_HEADS)[None, :]
    head_id = jnp.arange(F_W // 2) // F_DH
    hsum = (head_id[:, None] == head_id[None, :]).astype(BF16)

    assert PROJ_ROWS == FOX_TQ
    sel, aug1 = _bias_column_constants()
    u, kt, gc, gtm, qaug, kaug = _proj(x, mod3, norm_g[None, :], w_m, w_f, w_gate, conv_w, conv_b[None, :],
                                       gate_b, qg, kg, hsum, sel, aug1)
    yf = _fox(u, qaug, kaug)
    return _mlstm_out(u, kt, gc, gtm, mlstm_norm_g[None, :], x, mod3, yf, w_out.astype(BF16))


def kernel(x, c, norm_g, w_ada, b_ada, w_in, conv_w, conv_b, b_igate, b_fgate_m, mlstm_norm_g,
           b_fgate_f, fox_qnorm_g, fox_knorm_g, w_out):
    depth = norm_g.shape[0]
    b = x.shape[0]
    c_pad = jnp.pad(c, ((0, (-b) % SUBLANES), (0, 0)))
    params = (norm_g, w_ada, b_ada, w_in, conv_w, conv_b, b_igate, b_fgate_m, mlstm_norm_g, b_fgate_f,
              fox_qnorm_g, fox_knorm_g, w_out)
    for l in range(depth):
        layer = [p.reshape(p.shape[1:]) if depth == 1 else p[l] for p in params]
        x = _layer(x, c_pad, *layer)
    return x
```
